```python
import math
import jax, jax.numpy as jnp
from jax import lax
import numpy as np

D_MODEL = 2048
BATCH = 8
SEQ = 2048
DEPTH = 2

MLSTM_HEADS = 8
MLSTM_DH = 128
MLSTM_W = MLSTM_HEADS * MLSTM_DH
MLSTM_CHUNK = 64
QK_CONV = 4
NSA_HEADS = 8
NSA_KV_HEADS = 2
NSA_GROUP = NSA_HEADS // NSA_KV_HEADS
NSA_DH = 128
NSA_W = NSA_HEADS * NSA_DH
NSA_KV_W = NSA_KV_HEADS * NSA_DH
CMP_BLOCK = 32
CMP_STRIDE = 16
CMP_HIDDEN = 256
SEL_BLOCK = 64
SEL_TOPK = 16
SEL_QCHUNK = 64
WINDOW = 512
WIN_QBLOCK = 128
ROPE_THETA = 500000.0
ROPE_DIM = NSA_DH // 4
D_FF = 5632
FFN_CONV = 3
NORM_EPS = 1e-6
NEG = -1e30
FORCE_SCORE = 1e4

SPLIT_SIZES = (MLSTM_W, MLSTM_W, MLSTM_W, MLSTM_W, MLSTM_HEADS, MLSTM_HEADS,
               NSA_W, NSA_KV_W, NSA_KV_W, NSA_KV_W, NSA_KV_W, NSA_KV_W, NSA_KV_W,
               3 * NSA_HEADS)
N_IN = 4 * MLSTM_W + 2 * MLSTM_HEADS + NSA_W + 6 * NSA_KV_W + 3 * NSA_HEADS

kernel_name = "hymba_mlstm_nsa_convffn"


def rmsnorm(x, g):
    xf = x.astype(jnp.float32)
    y = xf * lax.rsqrt(jnp.mean(xf * xf, axis=-1, keepdims=True) + NORM_EPS)
    return (y * g.astype(jnp.float32)).astype(x.dtype)


def split_cols(z):
    pieces, start = [], 0
    for size in SPLIT_SIZES:
        pieces.append(z[..., start:start + size])
        start += size
    return pieces


def causal_dwconv(x, w, b):
    K, S = w.shape[0], x.shape[1]
    xp = jnp.pad(x, ((0, 0), (K - 1, 0), (0, 0)))
    y = b
    for j in range(K):
        y = y + xp[:, j:j + S] * w[j]
    return y


def rope_tables(seq):
    pos = jnp.arange(seq, dtype=jnp.float32)
    inv = ROPE_THETA ** (-jnp.arange(0, ROPE_DIM, 2, dtype=jnp.float32) / ROPE_DIM)
    ang = pos[:, None] * inv[None, :]
    return jnp.cos(ang), jnp.sin(ang)


def partial_rope(x, cos, sin):
    half = ROPE_DIM // 2
    c = cos[None, :, None, :].astype(x.dtype)
    s = sin[None, :, None, :].astype(x.dtype)
    x1, x2, rest = x[..., :half], x[..., half:ROPE_DIM], x[..., ROPE_DIM:]
    return jnp.concatenate([x1 * c - x2 * s, x2 * c + x1 * s, rest], axis=-1)


def mlstm_chunkwise(q, k, v, i_pre, f_pre):
    B, H, S, d = q.shape
    L = MLSTM_CHUNK
    nc = S // L
    qf = q.astype(jnp.float32)
    kf = k.astype(jnp.float32) * (d ** -0.5)
    vf = v.astype(jnp.float32)
    logf = jax.nn.log_sigmoid(f_pre)

    def chunks(a):
        return jnp.moveaxis(a.reshape((B, H, nc, L) + a.shape[3:]), 2, 0)

    causal = jnp.tril(jnp.ones((L, L), dtype=bool))

    def step(carry, inp):
        C, n, m = carry
        q_, k_, v_, i_, lf = inp
        b = jnp.cumsum(lf, axis=-1)
        a = b + m[..., None]
        D = b[..., :, None] - b[..., None, :] + i_[..., None, :]
        D = jnp.where(causal, D, -jnp.inf)
        m_t = jnp.maximum(a, jnp.max(D, axis=-1))
        w_inter = jnp.exp(a - m_t)
        W = jnp.exp(D - m_t[..., None])
        s_qk = jnp.einsum('bhtd,bhsd->bhts', q_, k_) * W
        num = (w_inter[..., None] * jnp.einsum('bhtd,bhvd->bhtv', q_, C)
               + jnp.einsum('bhts,bhsv->bhtv', s_qk, v_))
        den = w_inter * jnp.einsum('bhtd,bhd->bht', q_, n) + jnp.sum(s_qk, axis=-1)
        h = num / jnp.maximum(jnp.abs(den), jnp.exp(-m_t))[..., None]
        bL = b[..., -1]
        g = bL[..., None] - b + i_
        m_new = jnp.maximum(bL + m, jnp.max(g, axis=-1))
        decay = jnp.exp(bL + m - m_new)
        wk = jnp.exp(g - m_new[..., None])
        C_new = decay[..., None, None] * C + jnp.einsum('bhs,bhsv,bhsd->bhvd', wk, v_, k_)
        n_new = decay[..., None] * n + jnp.einsum('bhs,bhsd->bhd', wk, k_)
        return (C_new, n_new, m_new), h

    init = (jnp.zeros((B, H, d, d), jnp.float32), jnp.zeros((B, H, d), jnp.float32),
            jnp.zeros((B, H), jnp.float32))
    _, hc = lax.scan(step, init, (chunks(qf), chunks(kf), chunks(vf), chunks(i_pre), chunks(logf)))
    return jnp.moveaxis(hc, 0, 2).reshape(B, H, S, d)


def compress_blocks(k, pe, w1, w2):
    B, Hk, S, d = k.shape
    ncmp = (S - CMP_BLOCK) // CMP_STRIDE + 1
    idx = jnp.arange(ncmp)[:, None] * CMP_STRIDE + jnp.arange(CMP_BLOCK)[None, :]
    blk = k[:, :, idx] + pe.astype(k.dtype)
    blk = blk.reshape(B, Hk, ncmp, CMP_BLOCK * d)
    return jax.nn.gelu(blk @ w1) @ w2


def nsa_group(q, k_cmp, v_cmp, k_slc, v_slc, k_win, v_win, gate_pre,
              cmp_pe_k, cmp_pe_v, cmp_w1_k, cmp_w2_k, cmp_w1_v, cmp_w2_v):
    B, S, _, d = q.shape
    scale = d ** -0.5
    pos = jnp.arange(S)
    q5 = q.reshape(B, S, NSA_KV_HEADS, NSA_GROUP, d).transpose(0, 2, 3, 1, 4)
    heads = lambda a: a.transpose(0, 2, 1, 3)
    k_cmp, v_cmp, k_slc, v_slc, k_win, v_win = map(heads, (k_cmp, v_cmp, k_slc, v_slc, k_win, v_win))

    kc = compress_blocks(k_cmp, cmp_pe_k, cmp_w1_k, cmp_w2_k)
    vc = compress_blocks(v_cmp, cmp_pe_v, cmp_w1_v, cmp_w2_v)
    ncmp = kc.shape[2]
    block_end = jnp.arange(ncmp) * CMP_STRIDE + CMP_BLOCK - 1
    valid_c = block_end[None, :] <= pos[:, None]
    sc = jnp.einsum('bhgsd,bhcd->bhgsc', q5, kc).astype(jnp.float32) * scale
    p_cmp = jax.nn.softmax(jnp.where(valid_c, sc, NEG), axis=-1) * valid_c
    o_cmp = jnp.einsum('bhgsc,bhcd->bhgsd', p_cmp.astype(vc.dtype), vc)

    nsel = S // SEL_BLOCK
    n_top = min(SEL_TOPK, nsel)
    jb = jnp.arange(nsel)
    cstart = jnp.arange(ncmp) * CMP_STRIDE
    overlap = ((cstart[:, None] < (jb[None, :] + 1) * SEL_BLOCK)
               & (cstart[:, None] + CMP_BLOCK > jb[None, :] * SEL_BLOCK)).astype(jnp.float32)
    imp = jnp.einsum('bhgsc,cj->bhsj', p_cmp, overlap)
    cur = pos // SEL_BLOCK
    forced = (jb[None, :] == 0) | (jb[None, :] == cur[:, None]) | (jb[None, :] == cur[:, None] - 1)
    imp = jnp.where(forced, FORCE_SCORE, imp)
    imp = jnp.where(jb[None, :] <= cur[:, None], imp, -1.0)
    _, sel_idx = lax.top_k(imp, n_top)

    kb = k_slc.reshape(B, NSA_KV_HEADS, nsel, SEL_BLOCK, d)
    vb = v_slc.reshape(B, NSA_KV_HEADS, nsel, SEL_BLOCK, d)
    gather = jax.vmap(jax.vmap(lambda blocks, idx: blocks[idx]))
    QC = SEL_QCHUNK

    def sel_chunk(c):
        s0 = c * QC
        qc = lax.dynamic_slice_in_dim(q5, s0, QC, axis=3)
        ic = lax.dynamic_slice_in_dim(sel_idx, s0, QC, axis=2)
        ks = gather(kb, ic).reshape(B, NSA_KV_HEADS, QC, n_top * SEL_BLOCK, d)
        vs = gather(vb, ic).reshape(B, NSA_KV_HEADS, QC, n_top * SEL_BLOCK, d)
        kpos = (ic[..., None] * SEL_BLOCK + jnp.arange(SEL_BLOCK)).reshape(B, NSA_KV_HEADS, QC, n_top * SEL_BLOCK)
        qpos = s0 + jnp.arange(QC)
        mask = kpos <= qpos[None, None, :, None]
        s = jnp.einsum('bhgqd,bhqkd->bhgqk', qc, ks).astype(jnp.float32) * scale
        p = jax.nn.softmax(jnp.where(mask[:, :, None], s, NEG), axis=-1)
        return jnp.einsum('bhgqk,bhqkd->bhgqd', p.astype(vs.dtype), vs)

    o_slc = lax.map(sel_chunk, jnp.arange(S // QC))
    o_slc = jnp.moveaxis(o_slc, 0, 3).reshape(B, NSA_KV_HEADS, NSA_GROUP, S, d)

    QB, KW = WIN_QBLOCK, WINDOW + WIN_QBLOCK
    kp = jnp.pad(k_win, ((0, 0), (0, 0), (WINDOW, 0), (0, 0)))
    vp = jnp.pad(v_win, ((0, 0), (0, 0), (WINDOW, 0), (0, 0)))

    def win_block(c):
        s0 = c * QB
        qb = lax.dynamic_slice_in_dim(q5, s0, QB, axis=3)
        kw = lax.dynamic_slice_in_dim(kp, s0, KW, axis=2)
        vw = lax.dynamic_slice_in_dim(vp, s0, KW, axis=2)
        kpos = s0 - WINDOW + jnp.arange(KW)
        diff = (s0 + jnp.arange(QB))[:, None] - kpos[None, :]
        mask = (kpos[None, :] >= 0) & (diff >= 0) & (diff < WINDOW)
        s = jnp.einsum('bhgqd,bhkd->bhgqk', qb, kw).astype(jnp.float32) * scale
        p = jax.nn.softmax(jnp.where(mask, s, NEG), axis=-1)
        return jnp.einsum('bhgqk,bhkd->bhgqd', p.astype(vw.dtype), vw)

    o_win = lax.map(win_block, jnp.arange(S // QB))
    o_win = jnp.moveaxis(o_win, 0, 3).reshape(B, NSA_KV_HEADS, NSA_GROUP, S, d)

    g = jax.nn.sigmoid(gate_pre.astype(jnp.float32)).reshape(B, S, NSA_KV_HEADS, NSA_GROUP, 3)
    g = g.transpose(0, 2, 3, 1, 4).astype(q.dtype)
    o = g[..., 0:1] * o_cmp + g[..., 1:2] * o_slc + g[..., 2:3] * o_win
    return o.transpose(0, 3, 1, 2, 4).reshape(B, S, NSA_W)


def hybrid_mixer(h, w_in, qk_conv_w, qk_conv_b, i_bias, f_bias, mlstm_norm,
                 cmp_pe_k, cmp_pe_v, cmp_w1_k, cmp_w2_k, cmp_w1_v, cmp_w2_v,
                 nsa_norm, w_out, cos, sin):
    B, S, _ = h.shape
    (mq, mk, mv, mo, mi, mf, nq, kc, vc, ks, vs, kw, vw, ng) = split_cols(h @ w_in)

    qk = jax.nn.silu(causal_dwconv(jnp.concatenate([mq, mk], axis=-1), qk_conv_w, qk_conv_b))
    mq, mk = qk[..., :MLSTM_W], qk[..., MLSTM_W:]
    to_h = lambda a: a.reshape(B, S, MLSTM_HEADS, MLSTM_DH).transpose(0, 2, 1, 3)
    i_pre = (mi.astype(jnp.float32) + i_bias.astype(jnp.float32)).transpose(0, 2, 1)
    f_pre = (mf.astype(jnp.float32) + f_bias.astype(jnp.float32)).transpose(0, 2, 1)
    hm = mlstm_chunkwise(to_h(mq), to_h(mk), to_h(mv), i_pre, f_pre).transpose(0, 2, 1, 3)
    hm = rmsnorm(hm, mlstm_norm.reshape(MLSTM_HEADS, MLSTM_DH)).reshape(B, S, MLSTM_W)
    mix_a = hm.astype(h.dtype) * jax.nn.sigmoid(mo)

    q = partial_rope(nq.reshape(B, S, NSA_HEADS, NSA_DH), cos, sin)
    kv = lambda a: a.reshape(B, S, NSA_KV_HEADS, NSA_DH)
    rk = lambda a: partial_rope(kv(a), cos, sin)
    mix_b = nsa_group(q, rk(kc), kv(vc), rk(ks), kv(vs), rk(kw), kv(vw), ng,
                      cmp_pe_k, cmp_pe_v, cmp_w1_k, cmp_w2_k, cmp_w1_v, cmp_w2_v)
    mix_b = rmsnorm(mix_b, nsa_norm)

    return jnp.concatenate([mix_a, mix_b], axis=-1) @ w_out


def conv_gated_mlp(h, w_up, conv_w, conv_b, w_down):
    u = causal_dwconv(h @ w_up, conv_w, conv_b)
    gate, up = u[..., :D_FF], u[..., D_FF:]
    return (jax.nn.silu(gate) * up) @ w_down


def setup_inputs(seed: int = 0) -> dict:
    key = jax.random.key(seed)
    ks = jax.random.split(key, 32)
    f32 = jnp.float32
    nrm = lambda k, shape, fan_in: jax.random.normal(k, shape, f32) * (fan_in ** -0.5)
    gain = lambda k, shape: 1.0 + 0.02 * jax.random.normal(k, shape, f32)
    small = lambda k, shape, s: s * jax.random.normal(k, shape, f32)
    return {
        "x": jax.random.normal(ks[0], (BATCH, SEQ, D_MODEL), f32),
        "attn_norm": gain(ks[1], (DEPTH, D_MODEL)),
        "w_in": nrm(ks[2], (DEPTH, D_MODEL, N_IN), D_MODEL),
        "qk_conv_w": nrm(ks[3], (DEPTH, QK_CONV, 2 * MLSTM_W), QK_CONV),
        "qk_conv_b": small(ks[4], (DEPTH, 2 * MLSTM_W), 0.02),
        "i_bias": small(ks[5], (DEPTH, MLSTM_HEADS), 0.1),
        "f_bias": 3.0 + 3.0 * jax.random.uniform(ks[6], (DEPTH, MLSTM_HEADS), f32),
        "mlstm_norm": gain(ks[7], (DEPTH, MLSTM_W)),
        "cmp_pe_k": small(ks[8], (DEPTH, CMP_BLOCK, NSA_DH), 0.1),
        "cmp_pe_v": small(ks[9], (DEPTH, CMP_BLOCK, NSA_DH), 0.1),
        "cmp_w1_k": nrm(ks[10], (DEPTH, CMP_BLOCK * NSA_DH, CMP_HIDDEN), CMP_BLOCK * NSA_DH),
        "cmp_w2_k": nrm(ks[11], (DEPTH, CMP_HIDDEN, NSA_DH), CMP_HIDDEN),
        "cmp_w1_v": nrm(ks[12], (DEPTH, CMP_BLOCK * NSA_DH, CMP_HIDDEN), CMP_BLOCK * NSA_DH),
        "cmp_w2_v": nrm(ks[13], (DEPTH, CMP_HIDDEN, NSA_DH), CMP_HIDDEN),
        "nsa_norm": gain(ks[14], (DEPTH, NSA_W)),
        "w_out": nrm(ks[15], (DEPTH, D_MODEL, D_MODEL), D_MODEL),
        "ffn_norm": gain(ks[16], (DEPTH, D_MODEL)),
        "w_up": nrm(ks[17], (DEPTH, D_MODEL, 2 * D_FF), D_MODEL),
        "ffn_conv_w": nrm(ks[18], (DEPTH, FFN_CONV, 2 * D_FF), FFN_CONV),
        "ffn_conv_b": small(ks[19], (DEPTH, 2 * D_FF), 0.02),
        "w_down": nrm(ks[20], (DEPTH, D_FF, D_MODEL), D_FF),
        "final_norm": gain(ks[21], (D_MODEL,)),
    }


def reference(x, attn_norm, w_in, qk_conv_w, qk_conv_b, i_bias, f_bias, mlstm_norm,
              cmp_pe_k, cmp_pe_v, cmp_w1_k, cmp_w2_k, cmp_w1_v, cmp_w2_v, nsa_norm,
              w_out, ffn_norm, w_up, ffn_conv_w, ffn_conv_b, w_down, final_norm):
    cos, sin = rope_tables(x.shape[1])
    for l in range(DEPTH):
        h = rmsnorm(x, attn_norm[l])
        x = x + hybrid_mixer(h, w_in[l], qk_conv_w[l], qk_conv_b[l], i_bias[l], f_bias[l],
                             mlstm_norm[l], cmp_pe_k[l], cmp_pe_v[l], cmp_w1_k[l], cmp_w2_k[l],
                             cmp_w1_v[l], cmp_w2_v[l], nsa_norm[l], w_out[l], cos, sin)
        h = rmsnorm(x, ffn_norm[l])
        x = x + conv_gated_mlp(h, w_up[l], ffn_conv_w[l], ffn_conv_b[l], w_down[l])
    return rmsnorm(x, final_norm)
```

```python
import functools
import math

import jax
import jax.numpy as jnp
from jax import lax
from jax.experimental import pallas as pl
from jax.experimental.pallas import tpu as pltpu

D_MODEL = 2048
DEPTH = 2
MLSTM_HEADS = 8
HEAD_DIM = 128
MLSTM_W = MLSTM_HEADS * HEAD_DIM
QK_CONV = 4
NSA_HEADS = 8
NSA_KV_HEADS = 2
NSA_GROUP = NSA_HEADS // NSA_KV_HEADS
NSA_W = NSA_HEADS * HEAD_DIM
NSA_KV_W = NSA_KV_HEADS * HEAD_DIM
CMP_BLOCK = 32
CMP_STRIDE = 16
CMP_HIDDEN = 256
SEL_BLOCK = 64
SEL_TOPK = 16
WINDOW = 512
ROPE_THETA = 500000.0
ROPE_DIM = HEAD_DIM // 4
D_FF = 5632
FFN_CONV = 3
NORM_EPS = 1e-6
NEG = -1e30
FORCE_SCORE = 1e4
ATTN_SCALE = HEAD_DIM ** -0.5

LANES = 128
SUBLANES_F32 = 8
SUBLANES_BF16 = 16
VMEM_LIMIT_BYTES = 48 * 1024 * 1024

MM_TM = 512
MM_TN = 512
MLSTM_CHUNK = 128
MLSTM_HEADS_PER_STEP = 2
ATTN_TQ = 256
ATTN_TK = 256
WIN_KEYS = WINDOW + ATTN_TQ
PREP_TM = 512
FFN_HALO = SUBLANES_BF16

BF16 = jnp.bfloat16
F32 = jnp.float32

Z_MQ, Z_MK, Z_MV, Z_MO, Z_NQ = 0, 1024, 2048, 3072, 4096
Z_KC, Z_VC, Z_KS, Z_VS, Z_KW, Z_VW = 5120, 5376, 5632, 5888, 6144, 6400
Z_MAIN = 6656
ZG_MI, ZG_MF, ZG_NG = 0, 8, 16


def _cparams(*sem):
    return pltpu.CompilerParams(dimension_semantics=sem, vmem_limit_bytes=VMEM_LIMIT_BYTES)


def _rms(x, gain):
    return x * lax.rsqrt(jnp.mean(x * x, axis=-1, keepdims=True) + NORM_EPS) * gain


def _norm_matmul_kernel(x_ref, g_ref, w_ref, o_ref, h_ref):
    @pl.when(pl.program_id(1) == 0)
    def _():
        h_ref[...] = _rms(x_ref[...], g_ref[...]).astype(h_ref.dtype)

    o_ref[...] = jnp.dot(h_ref[...], w_ref[...], preferred_element_type=F32).astype(o_ref.dtype)


def norm_matmul(x, g, w, tn, out_dtype):
    m, k = x.shape
    n = w.shape[1]
    return pl.pallas_call(
        _norm_matmul_kernel,
        out_shape=jax.ShapeDtypeStruct((m, n), out_dtype),
        grid=(m // MM_TM, n // tn),
        in_specs=[
            pl.BlockSpec((MM_TM, k), lambda i, j: (i, 0)),
            pl.BlockSpec((1, k), lambda i, j: (0, 0)),
            pl.BlockSpec((k, tn), lambda i, j: (0, j)),
        ],
        out_specs=pl.BlockSpec((MM_TM, tn), lambda i, j: (i, j)),
        scratch_shapes=[pltpu.VMEM((MM_TM, k), BF16)],
        compiler_params=_cparams("parallel", "arbitrary"),
        name="norm_matmul",
    )(x, g.reshape(1, k), w)


def _matmul_res_kernel(n_lhs, *refs):
    lhs = refs[:n_lhs]
    ws = refs[n_lhs:2 * n_lhs]
    res_ref, o_ref = refs[2 * n_lhs], refs[2 * n_lhs + 1]
    acc = res_ref[...]
    for a_ref, w_ref in zip(lhs, ws):
        acc = acc + jnp.dot(a_ref[...], w_ref[...], preferred_element_type=F32)
    o_ref[...] = acc


def matmul_res(lhs_list, w_list, res):
    m, n = res.shape
    n_lhs = len(lhs_list)
    in_specs = [pl.BlockSpec((MM_TM, a.shape[1]), lambda i, j: (i, 0)) for a in lhs_list]
    in_specs += [pl.BlockSpec((w.shape[0], MM_TN), lambda i, j: (0, j)) for w in w_list]
    in_specs += [pl.BlockSpec((MM_TM, MM_TN), lambda i, j: (i, j))]
    return pl.pallas_call(
        functools.partial(_matmul_res_kernel, n_lhs),
        out_shape=jax.ShapeDtypeStruct((m, n), F32),
        grid=(m // MM_TM, n // MM_TN),
        in_specs=in_specs,
        out_specs=pl.BlockSpec((MM_TM, MM_TN), lambda i, j: (i, j)),
        compiler_params=_cparams("parallel", "arbitrary"),
        name="matmul_res",
    )(*lhs_list, *w_list, res)


def _mlstm_kernel(seq, zq_ref, zk_ref, zv_ref, zo_ref, gi_ref, gf_ref, bi_ref, bf_ref,
                  cwq_ref, cbq_ref, cwk_ref, cbk_ref, nrm_ref, o_ref,
                  pad_s, q_s, k_s, b_s, i_s, ct_s, n_s, m_s):
    L = MLSTM_CHUNK
    hps = MLSTM_HEADS_PER_STEP
    nchunks = seq // L
    pad = SUBLANES_F32

    def conv_silu(z_ref, cw_ref, cb_ref, dst_ref, scale):
        pad_s[0:pad, :] = jnp.zeros((pad, hps * HEAD_DIM), F32)
        pad_s[pad:pad + seq, :] = z_ref[...]
        for c in range(nchunks):
            acc = jnp.broadcast_to(cb_ref[...], (L, hps * HEAD_DIM))
            for j in range(QK_CONV):
                off = pad + c * L - (QK_CONV - 1) + j
                acc = acc + pad_s[off:off + L, :] * cw_ref[j:j + 1, :]
            y = acc * jax.nn.sigmoid(acc)
            dst_ref[c * L:(c + 1) * L, :] = (y * scale).astype(dst_ref.dtype)

    conv_silu(zq_ref, cwq_ref, cbq_ref, q_s, 1.0)
    conv_silu(zk_ref, cwk_ref, cbk_ref, k_s, ATTN_SCALE)

    lane = lax.broadcasted_iota(jnp.int32, (nchunks, L), 1)
    for hh in range(hps):
        i_s[hh] = gi_ref[0, hh] + bi_ref[hh]
        fpre = gf_ref[0, hh] + bf_ref[hh]
        csum = jnp.minimum(fpre, 0.0) - jnp.log1p(jnp.exp(-jnp.abs(fpre)))
        for sh in (1, 2, 4, 8, 16, 32, 64):
            csum = csum + jnp.where(lane >= sh, pltpu.roll(csum, sh, axis=1), 0.0)
        b_s[hh] = csum

    ct_s[...] = jnp.zeros_like(ct_s)
    n_s[...] = jnp.zeros_like(n_s)
    m_s[...] = jnp.zeros_like(m_s)

    row = lax.broadcasted_iota(jnp.int32, (L, L), 0)
    col = lax.broadcasted_iota(jnp.int32, (L, L), 1)
    eye = row == col
    causal = col <= row

    def chunk_body(c, carry):
        r0 = pl.multiple_of(c * L, L)
        for hh in range(hps):
            cs = slice(hh * HEAD_DIM, (hh + 1) * HEAD_DIM)
            qc = q_s[pl.ds(r0, L), cs]
            kc = k_s[pl.ds(r0, L), cs]
            vc = zv_ref[pl.ds(r0, L), cs].astype(BF16)
            b_row = b_s[hh, pl.ds(c, 1), :]
            i_row = i_s[hh, pl.ds(c, 1), :]
            b_col = jnp.sum(jnp.where(eye, b_row, 0.0), axis=-1, keepdims=True)
            i_col = jnp.sum(jnp.where(eye, i_row, 0.0), axis=-1, keepdims=True)
            m_prev = m_s[hh]
            ct = ct_s[hh]
            nvec = n_s[hh]

            a_col = b_col + m_prev
            dmat = jnp.where(causal, b_col - b_row + i_row, -jnp.inf)
            m_t = jnp.maximum(a_col, jnp.max(dmat, axis=-1, keepdims=True))
            w_inter = jnp.exp(a_col - m_t)
            wmat = jnp.exp(dmat - m_t)
            s_qk = lax.dot_general(qc, kc, (((1,), (1,)), ((), ())),
                                   preferred_element_type=F32) * wmat
            num = (w_inter * jnp.dot(qc, ct.astype(BF16), preferred_element_type=F32)
                   + jnp.dot(s_qk.astype(BF16), vc, preferred_element_type=F32))
            qn = jnp.sum(qc.astype(F32) * nvec, axis=-1, keepdims=True)
            den = w_inter * qn + jnp.sum(s_qk, axis=-1, keepdims=True)
            h = num / jnp.maximum(jnp.abs(den), jnp.exp(-m_t))

            b_last = b_col[L - 1:L, :]
            g_col = b_last - b_col + i_col
            g_row = b_last - b_row + i_row
            m_new = jnp.maximum(b_last + m_prev, jnp.max(g_row, axis=-1, keepdims=True))
            decay = jnp.exp(b_last + m_prev - m_new)
            kw = kc.astype(F32) * jnp.exp(g_col - m_new)
            ct_s[hh] = decay * ct + jnp.dot(kw.T.astype(BF16), vc, preferred_element_type=F32)
            n_s[hh] = decay * nvec + jnp.sum(kw, axis=0, keepdims=True)
            m_s[hh] = m_new

            hn = _rms(h, nrm_ref[:, cs])
            gate = jax.nn.sigmoid(zo_ref[pl.ds(r0, L), cs])
            o_ref[pl.ds(r0, L), cs] = (hn * gate).astype(o_ref.dtype)
        return carry

    lax.fori_loop(0, nchunks, chunk_body, 0)


def mlstm_group(z, gi, gf, bi, bf, conv_w, conv_b, norm_g, batch, seq):
    hps = MLSTM_HEADS_PER_STEP
    wblk = hps * HEAD_DIM
    nchunks = seq // MLSTM_CHUNK
    nhb = MLSTM_HEADS // hps
    zspec = lambda off: pl.BlockSpec((seq, wblk), lambda b, h, off=off: (b, off // wblk + h))
    gspec = pl.BlockSpec((1, hps, nchunks, MLSTM_CHUNK), lambda b, h: (b, h, 0, 0))
    bspec = pl.BlockSpec((hps, 1, MLSTM_CHUNK), lambda b, h: (h, 0, 0))
    return pl.pallas_call(
        functools.partial(_mlstm_kernel, seq),
        out_shape=jax.ShapeDtypeStruct((batch * seq, MLSTM_W), BF16),
        grid=(batch, nhb),
        in_specs=[
            zspec(Z_MQ), zspec(Z_MK), zspec(Z_MV), zspec(Z_MO),
            gspec, gspec, bspec, bspec,
            pl.BlockSpec((QK_CONV, wblk), lambda b, h: (0, h)),
            pl.BlockSpec((1, wblk), lambda b, h: (0, h)),
            pl.BlockSpec((QK_CONV, wblk), lambda b, h: (0, nhb + h)),
            pl.BlockSpec((1, wblk), lambda b, h: (0, nhb + h)),
            pl.BlockSpec((1, wblk), lambda b, h: (0, h)),
        ],
        out_specs=pl.BlockSpec((seq, wblk), lambda b, h: (b, h)),
        scratch_shapes=[
            pltpu.VMEM((seq + SUBLANES_F32, wblk), F32),
            pltpu.VMEM((seq, wblk), BF16),
            pltpu.VMEM((seq, wblk), BF16),
            pltpu.VMEM((hps, nchunks, MLSTM_CHUNK), F32),
            pltpu.VMEM((hps, nchunks, MLSTM_CHUNK), F32),
            pltpu.VMEM((hps, HEAD_DIM, HEAD_DIM), F32),
            pltpu.VMEM((hps, 1, HEAD_DIM), F32),
            pltpu.VMEM((hps, 1, 1), F32),
        ],
        compiler_params=_cparams("parallel", "parallel"),
        name="mlstm_group",
    )(z, z, z, z, gi, gf, bi, bf, conv_w, conv_b.reshape(1, -1), conv_w, conv_b.reshape(1, -1),
      norm_g.reshape(1, -1))


def _rope(x, cosf, sinf, lane):
    rot = jnp.where(lane < ROPE_DIM // 2, pltpu.roll(x, LANES - ROPE_DIM // 2, axis=1),
                    pltpu.roll(x, ROPE_DIM // 2, axis=1))
    return x * cosf + rot * sinf


def _nsa_prep_kernel(zq_ref, kc_ref, ks_ref, vs_ref, kw_ref, vw_ref, cos_ref, sin_ref,
                     q_o, kc_o, ks_o, vs_o, kw_o, vw_o):
    cosf, sinf = cos_ref[...], sin_ref[...]
    lane = lax.broadcasted_iota(jnp.int32, cosf.shape, 1)
    for h in range(NSA_HEADS):
        cs = slice(h * HEAD_DIM, (h + 1) * HEAD_DIM)
        q_o[:, cs] = _rope(zq_ref[:, cs], cosf, sinf, lane).astype(q_o.dtype)
    for h in range(NSA_KV_HEADS):
        cs = slice(h * HEAD_DIM, (h + 1) * HEAD_DIM)
        kc_o[:, cs] = _rope(kc_ref[:, cs], cosf, sinf, lane).astype(kc_o.dtype)
        ks_o[:, cs] = _rope(ks_ref[:, cs], cosf, sinf, lane).astype(ks_o.dtype)
        kw_o[:, cs] = _rope(kw_ref[:, cs], cosf, sinf, lane).astype(kw_o.dtype)
    vs_o[...] = vs_ref[...].astype(vs_o.dtype)
    vw_o[...] = vw_ref[...].astype(vw_o.dtype)


def nsa_prep(z, cosf, sinf, seq):
    m = z.shape[0]
    tm = PREP_TM
    nper = seq // tm
    kvspec = lambda off: pl.BlockSpec((tm, NSA_KV_W), lambda i, off=off: (i, off // NSA_KV_W))
    ospec = pl.BlockSpec((tm, NSA_KV_W), lambda i: (i, 0))
    tspec = pl.BlockSpec((tm, LANES), lambda i: (i % nper, 0))
    kv_bf = jax.ShapeDtypeStruct((m, NSA_KV_W), BF16)
    return pl.pallas_call(
        _nsa_prep_kernel,
        out_shape=(jax.ShapeDtypeStruct((m, NSA_W), BF16),
                   jax.ShapeDtypeStruct((m, NSA_KV_W), F32), kv_bf, kv_bf, kv_bf, kv_bf),
        grid=(m // tm,),
        in_specs=[pl.BlockSpec((tm, NSA_W), lambda i: (i, Z_NQ // NSA_W)),
                  kvspec(Z_KC), kvspec(Z_KS), kvspec(Z_VS), kvspec(Z_KW), kvspec(Z_VW),
                  tspec, tspec],
        out_specs=(pl.BlockSpec((tm, NSA_W), lambda i: (i, 0)), ospec, ospec, ospec, ospec, ospec),
        compiler_params=_cparams("parallel"),
        name="nsa_prep",
    )(z, z, z, z, z, z, cosf, sinf)


def _compress_kernel(x_ref, pe_ref, w1_ref, w2_ref, o_ref, sh_s):
    half = CMP_STRIDE * HEAD_DIM
    ngroups = x_ref.shape[1]
    x = x_ref[0]
    lo = jnp.dot((x + pe_ref[:, :half]).astype(BF16), w1_ref[:half, :], preferred_element_type=F32)
    hi = jnp.dot((x + pe_ref[:, half:]).astype(BF16), w1_ref[half:, :], preferred_element_type=F32)
    sh_s[0:ngroups, :] = hi
    sh_s[ngroups:ngroups + SUBLANES_F32, :] = jnp.zeros((SUBLANES_F32, CMP_HIDDEN), F32)
    pre = lo + sh_s[1:ngroups + 1, :]
    act = jax.nn.gelu(pre, approximate=True)
    o_ref[0] = jnp.dot(act.astype(BF16), w2_ref[...], preferred_element_type=F32)


def compress(xg, pe, w1, w2):
    nslab, ngroups, gw = xg.shape
    return pl.pallas_call(
        _compress_kernel,
        out_shape=jax.ShapeDtypeStruct((nslab, ngroups, HEAD_DIM), F32),
        grid=(nslab,),
        in_specs=[pl.BlockSpec((1, ngroups, gw), lambda s: (s, 0, 0)),
                  pl.BlockSpec((1, 2 * gw), lambda s: (0, 0)),
                  pl.BlockSpec((2 * gw, CMP_HIDDEN), lambda s: (0, 0)),
                  pl.BlockSpec((CMP_HIDDEN, HEAD_DIM), lambda s: (0, 0))],
        out_specs=pl.BlockSpec((1, ngroups, HEAD_DIM), lambda s: (s, 0, 0)),
        scratch_shapes=[pltpu.VMEM((ngroups + SUBLANES_F32, CMP_HIDDEN), F32)],
        compiler_params=_cparams("parallel"),
        name="nsa_compress",
    )(xg, pe.reshape(1, -1), w1, w2)


def _cmp_select_kernel(nsel, q_ref, kc_ref, vc_ref, ov_ref, o_ref, sel_ref):
    tq = ATTN_TQ
    s0 = pl.program_id(2) * tq
    kc = kc_ref[0].astype(BF16)
    vc = vc_ref[0].astype(BF16)
    ncp = kc.shape[0]
    pos = s0 + lax.broadcasted_iota(jnp.int32, (tq, ncp), 0)
    cidx = lax.broadcasted_iota(jnp.int32, (tq, ncp), 1)
    valid = cidx * CMP_STRIDE + (CMP_BLOCK - 1) <= pos
    any_valid = s0 + lax.broadcasted_iota(jnp.int32, (tq, 1), 0) >= CMP_BLOCK - 1
    p_sum = jnp.zeros((tq, ncp), F32)
    for g in range(NSA_GROUP):
        cs = slice(g * HEAD_DIM, (g + 1) * HEAD_DIM)
        sc = lax.dot_general(q_ref[:, cs], kc, (((1,), (1,)), ((), ())),
                             preferred_element_type=F32) * ATTN_SCALE
        sc = jnp.where(valid, sc, NEG)
        e = jnp.where(valid, jnp.exp(sc - jnp.max(sc, axis=-1, keepdims=True)), 0.0)
        denom = jnp.sum(e, axis=-1, keepdims=True)
        p = e / jnp.where(any_valid, denom, 1.0)
        o_ref[:, cs] = jnp.dot(p.astype(BF16), vc, preferred_element_type=F32).astype(o_ref.dtype)
        p_sum = p_sum + p
    imp = jnp.dot(p_sum, ov_ref[...], preferred_element_type=F32, precision=lax.Precision.HIGHEST)
    jb = cidx
    cur = pos // SEL_BLOCK
    forced = (jb == 0) | (jb == cur) | (jb == cur - 1)
    imp = jnp.where(forced, FORCE_SCORE, imp)
    imp = jnp.where(jb <= cur, imp, -1.0)
    rank = jnp.zeros((tq, ncp), F32)
    for j2 in range(nsel):
        colv = jnp.broadcast_to(imp[:, j2:j2 + 1], (tq, ncp))
        before = (colv > imp) | ((colv == imp) & (jb > j2))
        rank = rank + jnp.where(before, 1.0, 0.0)
    sel_ref[...] = jnp.where(rank < float(SEL_TOPK), 1.0, 0.0).astype(sel_ref.dtype)


def cmp_select(q, kcmp, vcmp, overlap, batch, seq):
    m = q.shape[0]
    tq = ATTN_TQ
    nq = seq // tq
    ncp = kcmp.shape[1]
    return pl.pallas_call(
        functools.partial(_cmp_select_kernel, seq // SEL_BLOCK),
        out_shape=(jax.ShapeDtypeStruct((m, NSA_W), F32),
                   jax.ShapeDtypeStruct((m, NSA_KV_HEADS * LANES), BF16)),
        grid=(batch, NSA_KV_HEADS, nq),
        in_specs=[pl.BlockSpec((tq, NSA_GROUP * HEAD_DIM), lambda b, h, i: (b * nq + i, h)),
                  pl.BlockSpec((1, ncp, HEAD_DIM), lambda b, h, i: (b * NSA_KV_HEADS + h, 0, 0)),
                  pl.BlockSpec((1, ncp, HEAD_DIM), lambda b, h, i: (b * NSA_KV_HEADS + h, 0, 0)),
                  pl.BlockSpec((ncp, LANES), lambda b, h, i: (0, 0))],
        out_specs=(pl.BlockSpec((tq, NSA_GROUP * HEAD_DIM), lambda b, h, i: (b * nq + i, h)),
                   pl.BlockSpec((tq, LANES), lambda b, h, i: (b * nq + i, h))),
        compiler_params=_cparams("parallel", "parallel", "parallel"),
        name="nsa_cmp_select",
    )(q, kcmp, vcmp, overlap)


def _sel_attn_kernel(q_ref, k_ref, v_ref, sel_ref, o_ref, m_s, l_s, acc_s):
    tq, tk = ATTN_TQ, ATTN_TK
    i = pl.program_id(2)
    s0 = i * tq
    m_s[...] = jnp.full_like(m_s, NEG)
    l_s[...] = jnp.zeros_like(l_s)
    acc_s[...] = jnp.zeros_like(acc_s)
    sel = sel_ref[...]
    blk_row = lax.broadcasted_iota(jnp.int32, (LANES, tk), 0)
    blk_of_key = lax.broadcasted_iota(jnp.int32, (LANES, tk), 1) // SEL_BLOCK
    qpos = s0 + lax.broadcasted_iota(jnp.int32, (tq, tk), 0)
    kcol = lax.broadcasted_iota(jnp.int32, (tq, tk), 1)

    def body(kt, carry):
        k0 = pl.multiple_of(kt * tk, tk)
        kt_blocks = kt * (tk // SEL_BLOCK)
        expand = jnp.where(blk_row == blk_of_key + kt_blocks, 1.0, 0.0).astype(BF16)
        chosen = jnp.dot(sel, expand, preferred_element_type=F32)
        allowed = (chosen > 0.5) & (k0 + kcol <= qpos)
        bias = jnp.where(allowed, 0.0, NEG)
        kt_ = k_ref[pl.ds(k0, tk), :]
        vt_ = v_ref[pl.ds(k0, tk), :]
        for g in range(NSA_GROUP):
            cs = slice(g * HEAD_DIM, (g + 1) * HEAD_DIM)
            s = lax.dot_general(q_ref[:, cs], kt_, (((1,), (1,)), ((), ())),
                                preferred_element_type=F32) * ATTN_SCALE + bias
            m_old = m_s[g]
            m_new = jnp.maximum(m_old, jnp.max(s, axis=-1, keepdims=True))
            alpha = jnp.exp(m_old - m_new)
            p = jnp.exp(s - m_new)
            l_s[g] = alpha * l_s[g] + jnp.sum(p, axis=-1, keepdims=True)
            acc_s[g] = alpha * acc_s[g] + jnp.dot(p.astype(BF16), vt_, preferred_element_type=F32)
            m_s[g] = m_new
        return carry

    lax.fori_loop(0, (s0 + tq) // tk, body, 0)
    for g in range(NSA_GROUP):
        cs = slice(g * HEAD_DIM, (g + 1) * HEAD_DIM)
        o_ref[:, cs] = (acc_s[g] / l_s[g]).astype(o_ref.dtype)


def sel_attn(q, ks, vs, sel, batch, seq):
    m = q.shape[0]
    tq = ATTN_TQ
    nq = seq // tq
    gw = NSA_GROUP * HEAD_DIM
    return pl.pallas_call(
        _sel_attn_kernel,
        out_shape=jax.ShapeDtypeStruct((m, NSA_W), F32),
        grid=(batch, NSA_KV_HEADS, nq),
        in_specs=[pl.BlockSpec((tq, gw), lambda b, h, i: (b * nq + i, h)),
                  pl.BlockSpec((seq, HEAD_DIM), lambda b, h, i: (b, h)),
                  pl.BlockSpec((seq, HEAD_DIM), lambda b, h, i: (b, h)),
                  pl.BlockSpec((tq, LANES), lambda b, h, i: (b * nq + i, h))],
        out_specs=pl.BlockSpec((tq, gw), lambda b, h, i: (b * nq + i, h)),
        scratch_shapes=[pltpu.VMEM((NSA_GROUP, tq, 1), F32),
                        pltpu.VMEM((NSA_GROUP, tq, 1), F32),
                        pltpu.VMEM((NSA_GROUP, tq, HEAD_DIM), F32)],
        compiler_params=_cparams("parallel", "parallel", "arbitrary"),
        name="nsa_sel_attn",
    )(q, ks, vs, sel)


def _win_attn_kernel(q_ref, k_ref, v_ref, o_ref):
    tq, nk = ATTN_TQ, WIN_KEYS
    s0 = pl.program_id(2) * tq
    start = pl.multiple_of(jnp.maximum(s0 - WINDOW, 0), tq)
    kw = k_ref[pl.ds(start, nk), :]
    vw = v_ref[pl.ds(start, nk), :]
    qpos = s0 + lax.broadcasted_iota(jnp.int32, (tq, nk), 0)
    kpos = start + lax.broadcasted_iota(jnp.int32, (tq, nk), 1)
    diff = qpos - kpos
    bias = jnp.where((diff >= 0) & (diff < WINDOW), 0.0, NEG)
    for g in range(NSA_GROUP):
        cs = slice(g * HEAD_DIM, (g + 1) * HEAD_DIM)
        s = lax.dot_general(q_ref[:, cs], kw, (((1,), (1,)), ((), ())),
                            preferred_element_type=F32) * ATTN_SCALE + bias
        p = jnp.exp(s - jnp.max(s, axis=-1, keepdims=True))
        denom = jnp.sum(p, axis=-1, keepdims=True)
        o = jnp.dot(p.astype(BF16), vw, preferred_element_type=F32)
        o_ref[:, cs] = (o / denom).astype(o_ref.dtype)


def win_attn(q, kw, vw, batch, seq):
    m = q.shape[0]
    tq = ATTN_TQ
    nq = seq // tq
    gw = NSA_GROUP * HEAD_DIM
    return pl.pallas_call(
        _win_attn_kernel,
        out_shape=jax.ShapeDtypeStruct((m, NSA_W), F32),
        grid=(batch, NSA_KV_HEADS, nq),
        in_specs=[pl.BlockSpec((tq, gw), lambda b, h, i: (b * nq + i, h)),
                  pl.BlockSpec((seq, HEAD_DIM), lambda b, h, i: (b, h)),
                  pl.BlockSpec((seq, HEAD_DIM), lambda b, h, i: (b, h))],
        out_specs=pl.BlockSpec((tq, gw), lambda b, h, i: (b * nq + i, h)),
        compiler_params=_cparams("parallel", "parallel", "parallel"),
        name="nsa_win_attn",
    )(q, kw, vw)


def _nsa_combine_kernel(oc_ref, os_ref, ow_ref, zg_ref, nrm_ref, o_ref, mix_s):
    tm = oc_ref.shape[0]
    gates = jax.nn.sigmoid(zg_ref[...])
    ssq = jnp.zeros((tm, 1), F32)
    for h in range(NSA_HEADS):
        cs = slice(h * HEAD_DIM, (h + 1) * HEAD_DIM)
        c0 = ZG_NG + 3 * h
        mix = (gates[:, c0:c0 + 1] * oc_ref[:, cs] + gates[:, c0 + 1:c0 + 2] * os_ref[:, cs]
               + gates[:, c0 + 2:c0 + 3] * ow_ref[:, cs])
        mix_s[:, cs] = mix
        ssq = ssq + jnp.sum(mix * mix, axis=-1, keepdims=True)
    inv = lax.rsqrt(ssq / NSA_W + NORM_EPS)
    o_ref[...] = (mix_s[...] * inv * nrm_ref[...]).astype(o_ref.dtype)


def nsa_combine(o_cmp, o_slc, o_win, zg, norm_g):
    m = o_cmp.shape[0]
    tm = PREP_TM
    ospec = pl.BlockSpec((tm, NSA_W), lambda i: (i, 0))
    return pl.pallas_call(
        _nsa_combine_kernel,
        out_shape=jax.ShapeDtypeStruct((m, NSA_W), BF16),
        grid=(m // tm,),
        in_specs=[ospec, ospec, ospec, pl.BlockSpec((tm, LANES), lambda i: (i, 0)),
                  pl.BlockSpec((1, NSA_W), lambda i: (0, 0))],
        out_specs=ospec,
        scratch_shapes=[pltpu.VMEM((tm, NSA_W), F32)],
        compiler_params=_cparams("parallel"),
        name="nsa_combine",
    )(o_cmp, o_slc, o_win, zg, norm_g.reshape(1, -1))


def _ffn_up_kernel(tiles_per_seq, x_ref, xp_ref, g_ref, wg_ref, wu_ref, cwg_ref, cwu_ref,
                   cbg_ref, cbu_ref, o_ref, h_s, ug_s, uu_s):
    tm = x_ref.shape[0]
    halo = FFN_HALO

    @pl.when(pl.program_id(1) == 0)
    def _():
        first = pl.program_id(0) % tiles_per_seq == 0
        hp = _rms(xp_ref[...], g_ref[...])
        h_s[0:halo, :] = jnp.where(first, 0.0, hp).astype(h_s.dtype)
        h_s[halo:halo + tm, :] = _rms(x_ref[...], g_ref[...]).astype(h_s.dtype)

    ug_s[...] = jnp.dot(h_s[...], wg_ref[...], preferred_element_type=F32)
    uu_s[...] = jnp.dot(h_s[...], wu_ref[...], preferred_element_type=F32)

    def conv(u_s, cw_ref, cb_ref):
        acc = jnp.broadcast_to(cb_ref[...], (tm, u_s.shape[1]))
        for j in range(FFN_CONV):
            off = halo - (FFN_CONV - 1) + j
            acc = acc + u_s[off:off + tm, :] * cw_ref[j:j + 1, :]
        return acc

    gate = conv(ug_s, cwg_ref, cbg_ref)
    up = conv(uu_s, cwu_ref, cbu_ref)
    o_ref[...] = (gate * jax.nn.sigmoid(gate) * up).astype(o_ref.dtype)


def ffn_up(x, g, w_up, conv_w, conv_b, seq):
    m, k = x.shape
    tm, tn, halo = MM_TM, MM_TN, FFN_HALO
    ntile = D_FF // tn
    cb = conv_b.reshape(1, -1)
    return pl.pallas_call(
        functools.partial(_ffn_up_kernel, seq // tm),
        out_shape=jax.ShapeDtypeStruct((m, D_FF), BF16),
        grid=(m // tm, ntile),
        in_specs=[
            pl.BlockSpec((tm, k), lambda i, j: (i, 0)),
            pl.BlockSpec((halo, k), lambda i, j: (jnp.maximum(i * (tm // halo) - 1, 0), 0)),
            pl.BlockSpec((1, k), lambda i, j: (0, 0)),
            pl.BlockSpec((k, tn), lambda i, j: (0, j)),
            pl.BlockSpec((k, tn), lambda i, j: (0, ntile + j)),
            pl.BlockSpec((FFN_CONV, tn), lambda i, j: (0, j)),
            pl.BlockSpec((FFN_CONV, tn), lambda i, j: (0, ntile + j)),
            pl.BlockSpec((1, tn), lambda i, j: (0, j)),
            pl.BlockSpec((1, tn), lambda i, j: (0, ntile + j)),
        ],
        out_specs=pl.BlockSpec((tm, tn), lambda i, j: (i, j)),
        scratch_shapes=[pltpu.VMEM((tm + halo, k), BF16),
                        pltpu.VMEM((tm + halo, tn), F32),
                        pltpu.VMEM((tm + halo, tn), F32)],
        compiler_params=_cparams("parallel", "arbitrary"),
        name="ffn_up",
    )(x, x, g.reshape(1, k), w_up, w_up, conv_w, conv_w, cb, cb)


def _rmsnorm_kernel(x_ref, g_ref, o_ref):
    o_ref[...] = _rms(x_ref[...], g_ref[...])


def rmsnorm(x, g):
    m, k = x.shape
    tm = PREP_TM
    return pl.pallas_call(
        _rmsnorm_kernel,
        out_shape=jax.ShapeDtypeStruct((m, k), F32),
        grid=(m // tm,),
        in_specs=[pl.BlockSpec((tm, k), lambda i: (i, 0)), pl.BlockSpec((1, k), lambda i: (0, 0))],
        out_specs=pl.BlockSpec((tm, k), lambda i: (i, 0)),
        compiler_params=_cparams("parallel"),
        name="final_rmsnorm",
    )(x, g.reshape(1, k))


def _split_w_in(w_in):
    sizes = (MLSTM_W, MLSTM_W, MLSTM_W, MLSTM_W, MLSTM_HEADS, MLSTM_HEADS,
             NSA_W, NSA_KV_W, NSA_KV_W, NSA_KV_W, NSA_KV_W, NSA_KV_W, NSA_KV_W, 3 * NSA_HEADS)
    offs = [0]
    for s in sizes:
        offs.append(offs[-1] + s)
    piece = lambda i: w_in[:, offs[i]:offs[i + 1]]
    main = jnp.concatenate([piece(i) for i in (0, 1, 2, 3, 6, 7, 8, 9, 10, 11, 12)], axis=1)
    gates = jnp.concatenate([piece(4), piece(5), piece(13)], axis=1)
    gates = jnp.pad(gates, ((0, 0), (0, LANES - gates.shape[1])))
    return main.astype(BF16), gates.astype(BF16)


def _rope_tables(seq):
    pos = jnp.arange(seq, dtype=F32)
    inv = ROPE_THETA ** (-jnp.arange(0, ROPE_DIM, 2, dtype=F32) / ROPE_DIM)
    ang = pos[:, None] * inv[None, :]
    cos, sin = jnp.cos(ang), jnp.sin(ang)
    rest = HEAD_DIM - ROPE_DIM
    cosf = jnp.concatenate([cos, cos, jnp.ones((seq, rest), F32)], axis=1)
    sinf = jnp.concatenate([-sin, sin, jnp.zeros((seq, rest), F32)], axis=1)
    return cosf, sinf


def _overlap_matrix(seq, ncmp_pad):
    nsel = seq // SEL_BLOCK
    cstart = jnp.arange(ncmp_pad) * CMP_STRIDE
    jb = jnp.arange(LANES)
    ov = ((cstart[:, None] < (jb[None, :] + 1) * SEL_BLOCK)
          & (cstart[:, None] + CMP_BLOCK > jb[None, :] * SEL_BLOCK) & (jb[None, :] < nsel))
    return ov.astype(F32)


def _to_groups(a, batch, seq):
    a = a.reshape(batch, seq, NSA_KV_HEADS, HEAD_DIM).transpose(0, 2, 1, 3)
    return a.reshape(batch * NSA_KV_HEADS, seq // CMP_STRIDE, CMP_STRIDE * HEAD_DIM)


def _layer(x, p, cosf, sinf, overlap, batch, seq):
    m = batch * seq
    w_main, w_gate = _split_w_in(p["w_in"])
    z = norm_matmul(x, p["attn_norm"], w_main, MM_TN, F32)
    zg = norm_matmul(x, p["attn_norm"], w_gate, LANES, F32)

    nchunks = seq // MLSTM_CHUNK
    gate_rows = lambda c0: (zg[:, c0:c0 + MLSTM_HEADS].reshape(batch, seq, MLSTM_HEADS)
                            .transpose(0, 2, 1).reshape(batch, MLSTM_HEADS, nchunks, MLSTM_CHUNK))
    lanes = lambda v: jnp.broadcast_to(v.reshape(MLSTM_HEADS, 1, 1), (MLSTM_HEADS, 1, MLSTM_CHUNK))
    mix_a = mlstm_group(z, gate_rows(ZG_MI), gate_rows(ZG_MF), lanes(p["i_bias"]), lanes(p["f_bias"]),
                        p["qk_conv_w"], p["qk_conv_b"], p["mlstm_norm"], batch, seq)

    q, kc, ks, vs, kw, vw = nsa_prep(z, cosf, sinf, seq)
    vc = z[:, Z_VC:Z_VC + NSA_KV_W]
    kcmp = compress(_to_groups(kc, batch, seq), p["cmp_pe_k"], p["cmp_w1_k"].astype(BF16),
                    p["cmp_w2_k"].astype(BF16))
    vcmp = compress(_to_groups(vc, batch, seq), p["cmp_pe_v"], p["cmp_w1_v"].astype(BF16),
                    p["cmp_w2_v"].astype(BF16))
    o_cmp, sel = cmp_select(q, kcmp, vcmp, overlap, batch, seq)
    o_slc = sel_attn(q, ks, vs, sel, batch, seq)
    o_win = win_attn(q, kw, vw, batch, seq)
    mix_b = nsa_combine(o_cmp, o_slc, o_win, zg, p["nsa_norm"])

    w_out = p["w_out"].astype(BF16)
    x = matmul_res([mix_a, mix_b], [w_out[:MLSTM_W], w_out[MLSTM_W:]], x)

    act = ffn_up(x, p["ffn_norm"], p["w_up"].astype(BF16), p["ffn_conv_w"], p["ffn_conv_b"], seq)
    x = matmul_res([act], [p["w_down"].astype(BF16)], x)
    return x


_LAYER_PARAMS = ("attn_norm", "w_in", "qk_conv_w", "qk_conv_b", "i_bias", "f_bias", "mlstm_norm",
                 "cmp_pe_k", "cmp_pe_v", "cmp_w1_k", "cmp_w2_k", "cmp_w1_v", "cmp_w2_v", "nsa_norm",
                 "w_out", "ffn_norm", "w_up", "ffn_conv_w", "ffn_conv_b", "w_down")


def kernel(x, attn_norm, w_in, qk_conv_w, qk_conv_b, i_bias, f_bias, mlstm_norm, cmp_pe_k, cmp_pe_v,
           cmp_w1_k, cmp_w2_k, cmp_w1_v, cmp_w2_v, nsa_norm, w_out, ffn_norm, w_up, ffn_conv_w,
           ffn_conv_b, w_down, final_norm):
    stacked = dict(zip(_LAYER_PARAMS, (attn_norm, w_in, qk_conv_w, qk_conv_b, i_bias, f_bias,
                                       mlstm_norm, cmp_pe_k, cmp_pe_v, cmp_w1_k, cmp_w2_k, cmp_w1_v,
                                       cmp_w2_v, nsa_norm, w_out, ffn_norm, w_up, ffn_conv_w,
                                       ffn_conv_b, w_down)))
    batch, seq, d = x.shape
    assert d == D_MODEL and seq % MM_TM == 0 and seq % ATTN_TQ == 0 and seq >= WIN_KEYS
    cosf, sinf = _rope_tables(seq)
    overlap = _overlap_matrix(seq, seq // CMP_STRIDE)
    xf = x.reshape(batch * seq, d)
    for layer in range(DEPTH):
        p = {k: v[layer] for k, v in stacked.items()}
        xf = _layer(xf, p, cosf, sinf, overlap, batch, seq)
    return rmsnorm(xf, final_norm).reshape(batch, seq, d)
```

```python
import functools

import jax
import jax.numpy as jnp
from jax import lax
from jax.experimental import pallas as pl
from jax.experimental.pallas import tpu as pltpu

D_MODEL = 2048
DEPTH = 2
MLSTM_HEADS = 8
HEAD_DIM = 128
MLSTM_W = MLSTM_HEADS * HEAD_DIM
QK_CONV = 4
NSA_HEADS = 8
NSA_KV_HEADS = 2
NSA_GROUP = NSA_HEADS // NSA_KV_HEADS
NSA_W = NSA_HEADS * HEAD_DIM
NSA_KV_W = NSA_KV_HEADS * HEAD_DIM
CMP_BLOCK = 32
CMP_STRIDE = 16
CMP_HIDDEN = 256
SEL_BLOCK = 64
SEL_TOPK = 16
WINDOW = 512
ROPE_THETA = 500000.0
ROPE_DIM = HEAD_DIM // 4
D_FF = 5632
FFN_CONV = 3
NORM_EPS = 1e-6
NEG = -1e30
FORCE_SCORE = 1e4
ATTN_SCALE = HEAD_DIM ** -0.5

LANES = 128
SUBLANES_F32 = 8
SUBLANES_BF16 = 16
VMEM_LIMIT_BYTES = 48 * 1024 * 1024

MM_TM = 1024
MM_TN = 512
DOWN_TN = 256
MLSTM_CHUNK = 128
MLSTM_HEADS_PER_STEP = 4
ATTN_TQ = 256
ATTN_TK = 256
WIN_KEYS = WINDOW + ATTN_TQ
ROW_TM = 512
CONV_HALO = SUBLANES_BF16

BF16 = jnp.bfloat16
F32 = jnp.float32

W_IN_QK = (0, 2 * MLSTM_W)
W_IN_VO = (2 * MLSTM_W, 4 * MLSTM_W)
W_IN_MI = 4 * MLSTM_W
W_IN_NSA = (W_IN_MI + 2 * MLSTM_HEADS, W_IN_MI + 2 * MLSTM_HEADS + NSA_W + 6 * NSA_KV_W)
W_IN_NG = W_IN_NSA[1]
ZN_W = NSA_W + 6 * NSA_KV_W
ZN_KC, ZN_VC, ZN_KS, ZN_VS, ZN_KW, ZN_VW = (NSA_W + i * NSA_KV_W for i in range(6))
ZG_MI, ZG_MF, ZG_NG = 0, MLSTM_HEADS, 2 * MLSTM_HEADS


def _cparams(*sem):
    return pltpu.CompilerParams(dimension_semantics=sem, vmem_limit_bytes=VMEM_LIMIT_BYTES)


def _rms(x, gain):
    return x * lax.rsqrt(jnp.mean(x * x, axis=-1, keepdims=True) + NORM_EPS) * gain


def _nt_dot(a, b):
    return lax.dot_general(a, b, (((1,), (1,)), ((), ())), preferred_element_type=F32)


def _stack_heads(q_ref):
    return jnp.concatenate([q_ref[:, g * HEAD_DIM:(g + 1) * HEAD_DIM] for g in range(NSA_GROUP)], axis=0)


def _norm_matmul_kernel(x_ref, g_ref, w_ref, o_ref, h_ref):
    @pl.when(pl.program_id(1) == 0)
    def _():
        h_ref[...] = _rms(x_ref[...], g_ref[...]).astype(h_ref.dtype)

    o_ref[...] = jnp.dot(h_ref[...], w_ref[...], preferred_element_type=F32).astype(o_ref.dtype)


def norm_matmul(x, g, w, out_dtype):
    m, k = x.shape
    n = w.shape[1]
    tm, tn = MM_TM, MM_TN
    return pl.pallas_call(
        _norm_matmul_kernel,
        out_shape=jax.ShapeDtypeStruct((m, n), out_dtype),
        grid=(m // tm, n // tn),
        in_specs=[
            pl.BlockSpec((tm, k), lambda i, j: (i, 0)),
            pl.BlockSpec((1, k), lambda i, j: (0, 0)),
            pl.BlockSpec((k, tn), lambda i, j: (0, j)),
        ],
        out_specs=pl.BlockSpec((tm, tn), lambda i, j: (i, j)),
        scratch_shapes=[pltpu.VMEM((tm, k), BF16)],
        compiler_params=_cparams("parallel", "arbitrary"),
        name="norm_matmul",
    )(x, g.reshape(1, k), w)


def _fill_halo_h(tiles_per_seq, x_ref, xp_ref, g_ref, h_s):
    tm = x_ref.shape[0]
    first = pl.program_id(0) % tiles_per_seq == 0
    hp = _rms(xp_ref[...], g_ref[...])
    h_s[0:CONV_HALO, :] = jnp.where(first, 0.0, hp).astype(h_s.dtype)
    h_s[CONV_HALO:CONV_HALO + tm, :] = _rms(x_ref[...], g_ref[...]).astype(h_s.dtype)


def _causal_conv(u_s, cw_ref, cb_ref, taps, tm):
    acc = jnp.broadcast_to(cb_ref[...], (tm, u_s.shape[1]))
    for j in range(taps):
        off = CONV_HALO - (taps - 1) + j
        acc = acc + u_s[off:off + tm, :] * cw_ref[j:j + 1, :]
    return acc


def _proj_qk_kernel(tiles_per_seq, n_q_tiles, x_ref, xp_ref, g_ref, w_ref, wg_ref, cw_ref, cb_ref,
                    o_ref, zg_ref, h_s, u_s):
    tm = x_ref.shape[0]
    j = pl.program_id(1)

    @pl.when(j == 0)
    def _():
        _fill_halo_h(tiles_per_seq, x_ref, xp_ref, g_ref, h_s)
        zg_ref[...] = jnp.dot(h_s[CONV_HALO:CONV_HALO + tm, :], wg_ref[...], preferred_element_type=F32)

    u_s[...] = jnp.dot(h_s[...], w_ref[...], preferred_element_type=F32)
    y = _causal_conv(u_s, cw_ref, cb_ref, QK_CONV, tm)
    y = y * jax.nn.sigmoid(y)
    scale = jnp.where(j >= n_q_tiles, ATTN_SCALE, 1.0)
    o_ref[...] = (y * scale).astype(o_ref.dtype)


def proj_qk(x, g, w_qk, w_gate, conv_w, conv_b, seq):
    m, k = x.shape
    tm, tn, halo = MM_TM, MM_TN, CONV_HALO
    n = w_qk.shape[1]
    return pl.pallas_call(
        functools.partial(_proj_qk_kernel, seq // tm, MLSTM_W // tn),
        out_shape=(jax.ShapeDtypeStruct((m, n), BF16), jax.ShapeDtypeStruct((m, LANES), F32)),
        grid=(m // tm, n // tn),
        in_specs=[
            pl.BlockSpec((tm, k), lambda i, j: (i, 0)),
            pl.BlockSpec((halo, k), lambda i, j: (jnp.maximum(i * (tm // halo) - 1, 0), 0)),
            pl.BlockSpec((1, k), lambda i, j: (0, 0)),
            pl.BlockSpec((k, tn), lambda i, j: (0, j)),
            pl.BlockSpec((k, LANES), lambda i, j: (0, 0)),
            pl.BlockSpec((QK_CONV, tn), lambda i, j: (0, j)),
            pl.BlockSpec((1, tn), lambda i, j: (0, j)),
        ],
        out_specs=(pl.BlockSpec((tm, tn), lambda i, j: (i, j)),
                   pl.BlockSpec((tm, LANES), lambda i, j: (i, 0))),
        scratch_shapes=[pltpu.VMEM((tm + halo, k), BF16), pltpu.VMEM((tm + halo, tn), F32)],
        compiler_params=_cparams("parallel", "arbitrary"),
        name="proj_qk",
    )(x, x, g.reshape(1, k), w_qk, w_gate, conv_w, conv_b.reshape(1, -1))


def _rope(x, cosf, sinf, lane):
    rot = jnp.where(lane < ROPE_DIM // 2, pltpu.roll(x, LANES - ROPE_DIM // 2, axis=1),
                    pltpu.roll(x, ROPE_DIM // 2, axis=1))
    return x * cosf + rot * sinf


def _proj_nsa_kernel(n_q_tiles, x_ref, g_ref, w_ref, cos_ref, sin_ref, o_ref, h_s):
    j = pl.program_id(1)

    @pl.when(j == 0)
    def _():
        h_s[...] = _rms(x_ref[...], g_ref[...]).astype(h_s.dtype)

    u = jnp.dot(h_s[...], w_ref[...], preferred_element_type=F32)
    cosf, sinf = cos_ref[...], sin_ref[...]
    lane = lax.broadcasted_iota(jnp.int32, cosf.shape, 1)
    is_q = j < n_q_tiles
    for c in range(u.shape[1] // HEAD_DIM):
        cs = slice(c * HEAD_DIM, (c + 1) * HEAD_DIM)
        xh = u[:, cs]
        roped = _rope(xh, cosf, sinf, lane)
        if c < NSA_KV_HEADS:
            yh = roped * jnp.where(is_q, ATTN_SCALE, 1.0)
        else:
            yh = jnp.where(is_q, roped * ATTN_SCALE, xh)
        o_ref[:, cs] = yh.astype(o_ref.dtype)


def proj_nsa(x, g, w_nsa, cosf, sinf, seq):
    m, k = x.shape
    tm, tn = MM_TM, MM_TN
    n = w_nsa.shape[1]
    assert tn == 2 * NSA_KV_W
    nper = seq // tm
    return pl.pallas_call(
        functools.partial(_proj_nsa_kernel, NSA_W // tn),
        out_shape=jax.ShapeDtypeStruct((m, n), BF16),
        grid=(m // tm, n // tn),
        in_specs=[
            pl.BlockSpec((tm, k), lambda i, j: (i, 0)),
            pl.BlockSpec((1, k), lambda i, j: (0, 0)),
            pl.BlockSpec((k, tn), lambda i, j: (0, j)),
            pl.BlockSpec((tm, LANES), lambda i, j: (i % nper, 0)),
            pl.BlockSpec((tm, LANES), lambda i, j: (i % nper, 0)),
        ],
        out_specs=pl.BlockSpec((tm, tn), lambda i, j: (i, j)),
        scratch_shapes=[pltpu.VMEM((tm, k), BF16)],
        compiler_params=_cparams("parallel", "arbitrary"),
        name="proj_nsa",
    )(x, g.reshape(1, k), w_nsa, cosf, sinf)


def _matmul_res_kernel(n_lhs, *refs):
    lhs = refs[:n_lhs]
    ws = refs[n_lhs:2 * n_lhs]
    res_ref, o_ref = refs[2 * n_lhs], refs[2 * n_lhs + 1]
    acc = res_ref[...]
    for a_ref, w_ref in zip(lhs, ws):
        acc = acc + jnp.dot(a_ref[...], w_ref[...], preferred_element_type=F32)
    o_ref[...] = acc


def matmul_res(lhs_list, w_list, res, tn):
    m, n = res.shape
    n_lhs = len(lhs_list)
    in_specs = [pl.BlockSpec((MM_TM, a.shape[1]), lambda i, j: (i, 0)) for a in lhs_list]
    in_specs += [pl.BlockSpec((w.shape[0], tn), lambda i, j: (0, j)) for w in w_list]
    in_specs += [pl.BlockSpec((MM_TM, tn), lambda i, j: (i, j))]
    return pl.pallas_call(
        functools.partial(_matmul_res_kernel, n_lhs),
        out_shape=jax.ShapeDtypeStruct((m, n), F32),
        grid=(m // MM_TM, n // tn),
        in_specs=in_specs,
        out_specs=pl.BlockSpec((MM_TM, tn), lambda i, j: (i, j)),
        compiler_params=_cparams("parallel", "arbitrary"),
        name="matmul_res",
    )(*lhs_list, *w_list, res)


def _mlstm_kernel(seq, q_ref, k_ref, v_ref, zo_ref, gi_ref, gf_ref, bi_ref, bf_ref, nrm_ref, o_ref,
                  b_s, i_s, cx_s, m_s):
    L = MLSTM_CHUNK
    hps = MLSTM_HEADS_PER_STEP
    nchunks = seq // L

    lane = lax.broadcasted_iota(jnp.int32, (nchunks, L), 1)
    for hh in range(hps):
        i_s[hh] = gi_ref[0, hh] + bi_ref[hh]
        fpre = gf_ref[0, hh] + bf_ref[hh]
        csum = jnp.minimum(fpre, 0.0) - jnp.log1p(jnp.exp(-jnp.abs(fpre)))
        for sh in (1, 2, 4, 8, 16, 32, 64):
            csum = csum + jnp.where(lane >= sh, pltpu.roll(csum, sh, axis=1), 0.0)
        b_s[hh] = csum

    cx_s[...] = jnp.zeros_like(cx_s)
    m_s[...] = jnp.zeros_like(m_s)

    row = lax.broadcasted_iota(jnp.int32, (L, L), 0)
    col = lax.broadcasted_iota(jnp.int32, (L, L), 1)
    eye = row == col
    causal = col <= row
    ones = jnp.ones((L, LANES), BF16)

    def chunk_body(c, carry):
        r0 = pl.multiple_of(c * L, L)
        for hh in range(hps):
            cs = slice(hh * HEAD_DIM, (hh + 1) * HEAD_DIM)
            qc = q_ref[pl.ds(r0, L), cs]
            kc = k_ref[pl.ds(r0, L), cs]
            vx = jnp.concatenate([v_ref[pl.ds(r0, L), cs], ones], axis=1)
            b_row = b_s[hh, pl.ds(c, 1), :]
            u_row = i_s[hh, pl.ds(c, 1), :] - b_row
            b_col = jnp.sum(jnp.where(eye, b_row, 0.0), axis=-1, keepdims=True)
            m_prev = m_s[hh]
            cx = cx_s[hh]

            umat = jnp.where(causal, u_row, -jnp.inf)
            m_col = jnp.maximum(m_prev, jnp.max(umat, axis=-1, keepdims=True))
            w_inter = jnp.exp(m_prev - m_col)
            s_qk = _nt_dot(qc, kc) * jnp.exp(umat - m_col)
            nd = (w_inter * jnp.dot(qc, cx.astype(BF16), preferred_element_type=F32)
                  + jnp.dot(s_qk.astype(BF16), vx, preferred_element_type=F32))
            floor = jnp.exp(-(b_col + m_col))
            h = nd[:, :HEAD_DIM] / jnp.maximum(jnp.abs(nd[:, HEAD_DIM:]), floor)

            m_last = m_col[L - 1:L, :]
            kw_t = kc.astype(F32).T * jnp.exp(u_row - m_last)
            cx_s[hh] = (jnp.exp(m_prev - m_last) * cx
                        + jnp.dot(kw_t.astype(BF16), vx, preferred_element_type=F32))
            m_s[hh] = b_row[:, L - 1:L] + m_last

            hn = _rms(h, nrm_ref[:, cs])
            gate = jax.nn.sigmoid(zo_ref[pl.ds(r0, L), cs].astype(F32))
            o_ref[pl.ds(r0, L), cs] = (hn * gate).astype(o_ref.dtype)
        return carry

    lax.fori_loop(0, nchunks, chunk_body, 0)


def mlstm_group(zqk, zvo, gi, gf, bi, bf, norm_g, batch, seq):
    hps = MLSTM_HEADS_PER_STEP
    wblk = hps * HEAD_DIM
    nchunks = seq // MLSTM_CHUNK
    nhb = MLSTM_HEADS // hps
    lo = pl.BlockSpec((seq, wblk), lambda b, h: (b, h))
    hi = pl.BlockSpec((seq, wblk), lambda b, h: (b, nhb + h))
    gspec = pl.BlockSpec((1, hps, nchunks, MLSTM_CHUNK), lambda b, h: (b, h, 0, 0))
    bspec = pl.BlockSpec((hps, 1, MLSTM_CHUNK), lambda b, h: (h, 0, 0))
    return pl.pallas_call(
        functools.partial(_mlstm_kernel, seq),
        out_shape=jax.ShapeDtypeStruct((batch * seq, MLSTM_W), BF16),
        grid=(batch, nhb),
        in_specs=[lo, hi, lo, hi, gspec, gspec, bspec, bspec,
                  pl.BlockSpec((1, wblk), lambda b, h: (0, h))],
        out_specs=pl.BlockSpec((seq, wblk), lambda b, h: (b, h)),
        scratch_shapes=[
            pltpu.VMEM((hps, nchunks, MLSTM_CHUNK), F32),
            pltpu.VMEM((hps, nchunks, MLSTM_CHUNK), F32),
            pltpu.VMEM((hps, HEAD_DIM, 2 * HEAD_DIM), F32),
            pltpu.VMEM((hps, 1, 1), F32),
        ],
        compiler_params=_cparams("parallel", "parallel"),
        name="mlstm_group",
    )(zqk, zqk, zvo, zvo, gi, gf, bi, bf, norm_g.reshape(1, -1))


def _compress_kernel(x_ref, pe_ref, w1_ref, w2_ref, o_ref, sh_s):
    half = CMP_STRIDE * HEAD_DIM
    ngroups = x_ref.shape[1]
    x = x_ref[0].astype(F32)
    lo = jnp.dot((x + pe_ref[:, :half]).astype(BF16), w1_ref[:half, :], preferred_element_type=F32)
    hi = jnp.dot((x + pe_ref[:, half:]).astype(BF16), w1_ref[half:, :], preferred_element_type=F32)
    sh_s[0:ngroups, :] = hi
    sh_s[ngroups:ngroups + SUBLANES_F32, :] = jnp.zeros((SUBLANES_F32, CMP_HIDDEN), F32)
    pre = lo + sh_s[1:ngroups + 1, :]
    act = jax.nn.gelu(pre, approximate=True)
    o_ref[0] = jnp.dot(act.astype(BF16), w2_ref[...], preferred_element_type=F32).astype(o_ref.dtype)


def compress(xg, pe, w1, w2):
    nslab, ngroups, gw = xg.shape
    return pl.pallas_call(
        _compress_kernel,
        out_shape=jax.ShapeDtypeStruct((nslab, ngroups, HEAD_DIM), BF16),
        grid=(nslab,),
        in_specs=[pl.BlockSpec((1, ngroups, gw), lambda s: (s, 0, 0)),
                  pl.BlockSpec((1, 2 * gw), lambda s: (0, 0)),
                  pl.BlockSpec((2 * gw, CMP_HIDDEN), lambda s: (0, 0)),
                  pl.BlockSpec((CMP_HIDDEN, HEAD_DIM), lambda s: (0, 0))],
        out_specs=pl.BlockSpec((1, ngroups, HEAD_DIM), lambda s: (s, 0, 0)),
        scratch_shapes=[pltpu.VMEM((ngroups + SUBLANES_F32, CMP_HIDDEN), F32)],
        compiler_params=_cparams("parallel"),
        name="nsa_compress",
    )(xg, pe.reshape(1, -1), w1, w2)


def _cmp_select_kernel(nsel, q_ref, kc_ref, vc_ref, ov_ref, o_ref, sel_ref):
    tq = ATTN_TQ
    s0 = pl.program_id(2) * tq
    kc = kc_ref[0]
    vc = vc_ref[0]
    ncp = kc.shape[0]
    pos = s0 + lax.broadcasted_iota(jnp.int32, (tq, ncp), 0)
    cidx = lax.broadcasted_iota(jnp.int32, (tq, ncp), 1)
    valid = cidx * CMP_STRIDE + (CMP_BLOCK - 1) <= pos
    any_valid = s0 + lax.broadcasted_iota(jnp.int32, (tq, 1), 0) >= CMP_BLOCK - 1
    sc = _nt_dot(_stack_heads(q_ref), kc)
    sc = jnp.where(valid[None], sc.reshape(NSA_GROUP, tq, ncp), NEG)
    e = jnp.where(valid[None], jnp.exp(sc - jnp.max(sc, axis=-1, keepdims=True)), 0.0)
    denom = jnp.sum(e, axis=-1, keepdims=True)
    p = e / jnp.where(any_valid[None], denom, 1.0)
    o = jnp.dot(p.reshape(NSA_GROUP * tq, ncp).astype(BF16), vc, preferred_element_type=F32)
    p_sum = p[0]
    for g in range(NSA_GROUP):
        o_ref[:, g * HEAD_DIM:(g + 1) * HEAD_DIM] = o[g * tq:(g + 1) * tq, :].astype(o_ref.dtype)
        if g:
            p_sum = p_sum + p[g]
    imp = jnp.dot(p_sum, ov_ref[...], preferred_element_type=F32, precision=lax.Precision.HIGHEST)
    jb = cidx
    cur = pos // SEL_BLOCK
    forced = (jb == 0) | (jb == cur) | (jb == cur - 1)
    imp = jnp.where(forced, FORCE_SCORE, imp)
    imp = jnp.where(jb <= cur, imp, -1.0)
    imp_t = imp.T[:nsel, :]
    jb_t = lax.broadcasted_iota(jnp.int32, (nsel, tq), 0)
    rank = jnp.zeros((nsel, tq), F32)
    for j2 in range(nsel):
        rowv = imp_t[j2:j2 + 1, :]
        before = (rowv > imp_t) | ((rowv == imp_t) & (jb_t > j2))
        rank = rank + jnp.where(before, 1.0, 0.0)
    sel_t = jnp.where(rank < float(SEL_TOPK), 1.0, 0.0)
    sel_t = jnp.concatenate([sel_t, jnp.zeros((ncp - nsel, tq), F32)], axis=0)
    sel_ref[...] = sel_t.T.astype(sel_ref.dtype)


def cmp_select(zn, kcmp, vcmp, overlap, batch, seq):
    m = zn.shape[0]
    tq = ATTN_TQ
    nq = seq // tq
    ncp = kcmp.shape[1]
    gw = NSA_GROUP * HEAD_DIM
    return pl.pallas_call(
        functools.partial(_cmp_select_kernel, seq // SEL_BLOCK),
        out_shape=(jax.ShapeDtypeStruct((m, NSA_W), F32),
                   jax.ShapeDtypeStruct((m, NSA_KV_HEADS * LANES), BF16)),
        grid=(batch, NSA_KV_HEADS, nq),
        in_specs=[pl.BlockSpec((tq, gw), lambda b, h, i: (b * nq + i, h)),
                  pl.BlockSpec((1, ncp, HEAD_DIM), lambda b, h, i: (b * NSA_KV_HEADS + h, 0, 0)),
                  pl.BlockSpec((1, ncp, HEAD_DIM), lambda b, h, i: (b * NSA_KV_HEADS + h, 0, 0)),
                  pl.BlockSpec((ncp, LANES), lambda b, h, i: (0, 0))],
        out_specs=(pl.BlockSpec((tq, gw), lambda b, h, i: (b * nq + i, h)),
                   pl.BlockSpec((tq, LANES), lambda b, h, i: (b * nq + i, h))),
        compiler_params=_cparams("parallel", "parallel", "parallel"),
        name="nsa_cmp_select",
    )(zn, kcmp, vcmp, overlap)


def _sel_attn_kernel(q_ref, k_ref, v_ref, oh_ref, sel_ref, o_ref, qx_s, m_s, acc_s):
    tq, tk = ATTN_TQ, ATTN_TK
    i = pl.program_id(2)
    penalty = ((sel_ref[...].astype(F32) - 1.0) * (-NEG)).astype(BF16)
    for g in range(NSA_GROUP):
        cs = slice(g * HEAD_DIM, (g + 1) * HEAD_DIM)
        qx_s[g * tq:(g + 1) * tq, :] = jnp.concatenate([q_ref[:, cs], penalty], axis=1)
    m_s[...] = jnp.full_like(m_s, NEG)
    acc_s[...] = jnp.zeros_like(acc_s)
    ones = jnp.ones((tk, LANES), BF16)

    def step(k0, diagonal):
        kx = jnp.concatenate([k_ref[pl.ds(k0, tk), :], oh_ref[pl.ds(k0, tk), :]], axis=1)
        vx = jnp.concatenate([v_ref[pl.ds(k0, tk), :], ones], axis=1)
        s = _nt_dot(qx_s[...], kx)
        if diagonal:
            row = lax.broadcasted_iota(jnp.int32, (tq, tk), 0)
            col = lax.broadcasted_iota(jnp.int32, (tq, tk), 1)
            causal = jnp.where(col <= row, 0.0, NEG)
            s = (s.reshape(NSA_GROUP, tq, tk) + causal[None]).reshape(NSA_GROUP * tq, tk)
        m_old = m_s[...]
        m_new = jnp.maximum(m_old, jnp.max(s, axis=-1, keepdims=True))
        alpha = jnp.exp(m_old - m_new)
        p = jnp.exp((s - jnp.concatenate([m_new] * (tk // LANES), axis=1)).astype(BF16))
        acc_s[...] = (jnp.concatenate([alpha, alpha], axis=1) * acc_s[...]
                      + jnp.dot(p, vx, preferred_element_type=F32))
        m_s[...] = m_new

    def body(kt, carry):
        step(pl.multiple_of(kt * tk, tk), False)
        return carry

    lax.fori_loop(0, i, body, 0)
    step(pl.multiple_of(i * tk, tk), True)
    for g in range(NSA_GROUP):
        cs = slice(g * HEAD_DIM, (g + 1) * HEAD_DIM)
        acc = acc_s[g * tq:(g + 1) * tq, :]
        o_ref[:, cs] = (acc[:, :HEAD_DIM] / acc[:, HEAD_DIM:]).astype(o_ref.dtype)


def _kv_spec(seq, col0):
    return pl.BlockSpec((seq, HEAD_DIM), lambda b, h, i: (b, col0 // HEAD_DIM + h))


def sel_attn(zn, onehot, sel, batch, seq):
    assert ATTN_TQ == ATTN_TK
    m = zn.shape[0]
    tq = ATTN_TQ
    nq = seq // tq
    gw = NSA_GROUP * HEAD_DIM
    return pl.pallas_call(
        _sel_attn_kernel,
        out_shape=jax.ShapeDtypeStruct((m, NSA_W), F32),
        grid=(batch, NSA_KV_HEADS, nq),
        in_specs=[pl.BlockSpec((tq, gw), lambda b, h, i: (b * nq + i, h)),
                  _kv_spec(seq, ZN_KS), _kv_spec(seq, ZN_VS),
                  pl.BlockSpec((seq, LANES), lambda b, h, i: (0, 0)),
                  pl.BlockSpec((tq, LANES), lambda b, h, i: (b * nq + i, h))],
        out_specs=pl.BlockSpec((tq, gw), lambda b, h, i: (b * nq + i, h)),
        scratch_shapes=[pltpu.VMEM((NSA_GROUP * tq, 2 * LANES), BF16),
                        pltpu.VMEM((NSA_GROUP * tq, LANES), F32),
                        pltpu.VMEM((NSA_GROUP * tq, 2 * LANES), F32)],
        compiler_params=_cparams("parallel", "parallel", "arbitrary"),
        name="nsa_sel_attn",
    )(zn, zn, zn, onehot, sel)


def _win_attn_kernel(q_ref, k_ref, v_ref, o_ref):
    tq, nk = ATTN_TQ, WIN_KEYS
    s0 = pl.program_id(2) * tq
    start = pl.multiple_of(jnp.maximum(s0 - WINDOW, 0), tq)
    kw = k_ref[pl.ds(start, nk), :]
    vx = jnp.concatenate([v_ref[pl.ds(start, nk), :], jnp.ones((nk, LANES), BF16)], axis=1)
    qpos = s0 + lax.broadcasted_iota(jnp.int32, (tq, nk), 0)
    kpos = start + lax.broadcasted_iota(jnp.int32, (tq, nk), 1)
    diff = qpos - kpos
    bias = jnp.where((diff >= 0) & (diff < WINDOW), 0.0, NEG)
    s = _nt_dot(_stack_heads(q_ref), kw)
    s = (s.reshape(NSA_GROUP, tq, nk) + bias[None]).reshape(NSA_GROUP * tq, nk)
    p = jnp.exp((s - jnp.max(s, axis=-1, keepdims=True)).astype(BF16))
    acc = jnp.dot(p, vx, preferred_element_type=F32)
    for g in range(NSA_GROUP):
        a = acc[g * tq:(g + 1) * tq, :]
        o_ref[:, g * HEAD_DIM:(g + 1) * HEAD_DIM] = (a[:, :HEAD_DIM] / a[:, HEAD_DIM:]).astype(o_ref.dtype)


def win_attn(zn, batch, seq):
    m = zn.shape[0]
    tq = ATTN_TQ
    nq = seq // tq
    gw = NSA_GROUP * HEAD_DIM
    return pl.pallas_call(
        _win_attn_kernel,
        out_shape=jax.ShapeDtypeStruct((m, NSA_W), F32),
        grid=(batch, NSA_KV_HEADS, nq),
        in_specs=[pl.BlockSpec((tq, gw), lambda b, h, i: (b * nq + i, h)),
                  _kv_spec(seq, ZN_KW), _kv_spec(seq, ZN_VW)],
        out_specs=pl.BlockSpec((tq, gw), lambda b, h, i: (b * nq + i, h)),
        compiler_params=_cparams("parallel", "parallel", "parallel"),
        name="nsa_win_attn",
    )(zn, zn, zn)


def _nsa_combine_kernel(oc_ref, os_ref, ow_ref, zg_ref, nrm_ref, o_ref, mix_s):
    tm = oc_ref.shape[0]
    gates = jax.nn.sigmoid(zg_ref[...])
    ssq = jnp.zeros((tm, 1), F32)
    for h in range(NSA_HEADS):
        cs = slice(h * HEAD_DIM, (h + 1) * HEAD_DIM)
        c0 = ZG_NG + 3 * h
        mix = (gates[:, c0:c0 + 1] * oc_ref[:, cs] + gates[:, c0 + 1:c0 + 2] * os_ref[:, cs]
               + gates[:, c0 + 2:c0 + 3] * ow_ref[:, cs])
        mix_s[:, cs] = mix
        ssq = ssq + jnp.sum(mix * mix, axis=-1, keepdims=True)
    inv = lax.rsqrt(ssq / NSA_W + NORM_EPS)
    o_ref[...] = (mix_s[...] * inv * nrm_ref[...]).astype(o_ref.dtype)


def nsa_combine(o_cmp, o_slc, o_win, zg, norm_g):
    m = o_cmp.shape[0]
    tm = ROW_TM
    ospec = pl.BlockSpec((tm, NSA_W), lambda i: (i, 0))
    return pl.pallas_call(
        _nsa_combine_kernel,
        out_shape=jax.ShapeDtypeStruct((m, NSA_W), BF16),
        grid=(m // tm,),
        in_specs=[ospec, ospec, ospec, pl.BlockSpec((tm, LANES), lambda i: (i, 0)),
                  pl.BlockSpec((1, NSA_W), lambda i: (0, 0))],
        out_specs=ospec,
        scratch_shapes=[pltpu.VMEM((tm, NSA_W), F32)],
        compiler_params=_cparams("parallel"),
        name="nsa_combine",
    )(o_cmp, o_slc, o_win, zg, norm_g.reshape(1, -1))


def _ffn_up_kernel(tiles_per_seq, x_ref, xp_ref, g_ref, wg_ref, wu_ref, cwg_ref, cwu_ref,
                   cbg_ref, cbu_ref, o_ref, h_s, ug_s, uu_s):
    tm = x_ref.shape[0]

    @pl.when(pl.program_id(1) == 0)
    def _():
        _fill_halo_h(tiles_per_seq, x_ref, xp_ref, g_ref, h_s)

    ug_s[...] = jnp.dot(h_s[...], wg_ref[...], preferred_element_type=F32)
    uu_s[...] = jnp.dot(h_s[...], wu_ref[...], preferred_element_type=F32)
    gate = _causal_conv(ug_s, cwg_ref, cbg_ref, FFN_CONV, tm)
    up = _causal_conv(uu_s, cwu_ref, cbu_ref, FFN_CONV, tm)
    o_ref[...] = (gate * jax.nn.sigmoid(gate) * up).astype(o_ref.dtype)


def ffn_up(x, g, w_up, conv_w, conv_b, seq):
    m, k = x.shape
    tm, tn, halo = MM_TM, MM_TN, CONV_HALO
    ntile = D_FF // tn
    cb = conv_b.reshape(1, -1)
    return pl.pallas_call(
        functools.partial(_ffn_up_kernel, seq // tm),
        out_shape=jax.ShapeDtypeStruct((m, D_FF), BF16),
        grid=(m // tm, ntile),
        in_specs=[
            pl.BlockSpec((tm, k), lambda i, j: (i, 0)),
            pl.BlockSpec((halo, k), lambda i, j: (jnp.maximum(i * (tm // halo) - 1, 0), 0)),
            pl.BlockSpec((1, k), lambda i, j: (0, 0)),
            pl.BlockSpec((k, tn), lambda i, j: (0, j)),
            pl.BlockSpec((k, tn), lambda i, j: (0, ntile + j)),
            pl.BlockSpec((FFN_CONV, tn), lambda i, j: (0, j)),
            pl.BlockSpec((FFN_CONV, tn), lambda i, j: (0, ntile + j)),
            pl.BlockSpec((1, tn), lambda i, j: (0, j)),
            pl.BlockSpec((1, tn), lambda i, j: (0, ntile + j)),
        ],
        out_specs=pl.BlockSpec((tm, tn), lambda i, j: (i, j)),
        scratch_shapes=[pltpu.VMEM((tm + halo, k), BF16),
                        pltpu.VMEM((tm + halo, tn), F32),
                        pltpu.VMEM((tm + halo, tn), F32)],
        compiler_params=_cparams("parallel", "arbitrary"),
        name="ffn_up",
    )(x, x, g.reshape(1, k), w_up, w_up, conv_w, conv_w, cb, cb)


def _rmsnorm_kernel(x_ref, g_ref, o_ref):
    o_ref[...] = _rms(x_ref[...], g_ref[...])


def rmsnorm(x, g):
    m, k = x.shape
    tm = ROW_TM
    return pl.pallas_call(
        _rmsnorm_kernel,
        out_shape=jax.ShapeDtypeStruct((m, k), F32),
        grid=(m // tm,),
        in_specs=[pl.BlockSpec((tm, k), lambda i: (i, 0)), pl.BlockSpec((1, k), lambda i: (0, 0))],
        out_specs=pl.BlockSpec((tm, k), lambda i: (i, 0)),
        compiler_params=_cparams("parallel"),
        name="final_rmsnorm",
    )(x, g.reshape(1, k))


def _gate_weight(w_in):
    gates = jnp.concatenate([w_in[:, W_IN_MI:W_IN_MI + 2 * MLSTM_HEADS],
                             w_in[:, W_IN_NG:W_IN_NG + 3 * NSA_HEADS]], axis=1)
    return jnp.pad(gates, ((0, 0), (0, LANES - gates.shape[1]))).astype(BF16)


def _rope_tables(seq):
    pos = jnp.arange(seq, dtype=F32)
    inv = ROPE_THETA ** (-jnp.arange(0, ROPE_DIM, 2, dtype=F32) / ROPE_DIM)
    ang = pos[:, None] * inv[None, :]
    cos, sin = jnp.cos(ang), jnp.sin(ang)
    rest = HEAD_DIM - ROPE_DIM
    cosf = jnp.concatenate([cos, cos, jnp.ones((seq, rest), F32)], axis=1)
    sinf = jnp.concatenate([-sin, sin, jnp.zeros((seq, rest), F32)], axis=1)
    return cosf, sinf


def _overlap_matrix(seq, ncmp_pad):
    nsel = seq // SEL_BLOCK
    cstart = jnp.arange(ncmp_pad) * CMP_STRIDE
    jb = jnp.arange(LANES)
    ov = ((cstart[:, None] < (jb[None, :] + 1) * SEL_BLOCK)
          & (cstart[:, None] + CMP_BLOCK > jb[None, :] * SEL_BLOCK) & (jb[None, :] < nsel))
    return ov.astype(F32)


def _block_onehot(seq):
    blk = jnp.arange(seq)[:, None] // SEL_BLOCK
    return (blk == jnp.arange(LANES)[None, :]).astype(BF16)


def _to_groups(a, batch, seq):
    a = a.reshape(batch, seq, NSA_KV_HEADS, HEAD_DIM).transpose(0, 2, 1, 3)
    return a.reshape(batch * NSA_KV_HEADS, seq // CMP_STRIDE, CMP_STRIDE * HEAD_DIM)


def _layer(x, p, cosf, sinf, overlap, onehot, batch, seq):
    w_in = p["w_in"]
    w_qk = w_in[:, W_IN_QK[0]:W_IN_QK[1]].astype(BF16)
    w_vo = w_in[:, W_IN_VO[0]:W_IN_VO[1]].astype(BF16)
    w_nsa = w_in[:, W_IN_NSA[0]:W_IN_NSA[1]].astype(BF16)
    zqk, zg = proj_qk(x, p["attn_norm"], w_qk, _gate_weight(w_in), p["qk_conv_w"], p["qk_conv_b"], seq)
    zvo = norm_matmul(x, p["attn_norm"], w_vo, BF16)
    zn = proj_nsa(x, p["attn_norm"], w_nsa, cosf, sinf, seq)

    nchunks = seq // MLSTM_CHUNK
    gate_rows = lambda c0: (zg[:, c0:c0 + MLSTM_HEADS].reshape(batch, seq, MLSTM_HEADS)
                            .transpose(0, 2, 1).reshape(batch, MLSTM_HEADS, nchunks, MLSTM_CHUNK))
    lanes = lambda v: jnp.broadcast_to(v.reshape(MLSTM_HEADS, 1, 1), (MLSTM_HEADS, 1, MLSTM_CHUNK))
    mix_a = mlstm_group(zqk, zvo, gate_rows(ZG_MI), gate_rows(ZG_MF), lanes(p["i_bias"]),
                        lanes(p["f_bias"]), p["mlstm_norm"], batch, seq)

    kcmp = compress(_to_groups(zn[:, ZN_KC:ZN_KC + NSA_KV_W], batch, seq), p["cmp_pe_k"],
                    p["cmp_w1_k"].astype(BF16), p["cmp_w2_k"].astype(BF16))
    vcmp = compress(_to_groups(zn[:, ZN_VC:ZN_VC + NSA_KV_W], batch, seq), p["cmp_pe_v"],
                    p["cmp_w1_v"].astype(BF16), p["cmp_w2_v"].astype(BF16))
    o_cmp, sel = cmp_select(zn, kcmp, vcmp, overlap, batch, seq)
    o_slc = sel_attn(zn, onehot, sel, batch, seq)
    o_win = win_attn(zn, batch, seq)
    mix_b = nsa_combine(o_cmp, o_slc, o_win, zg, p["nsa_norm"])

    w_out = p["w_out"].astype(BF16)
    x = matmul_res([mix_a, mix_b], [w_out[:MLSTM_W], w_out[MLSTM_W:]], x, MM_TN)

    act = ffn_up(x, p["ffn_norm"], p["w_up"].astype(BF16), p["ffn_conv_w"], p["ffn_conv_b"], seq)
    x = matmul_res([act], [p["w_down"].astype(BF16)], x, DOWN_TN)
    return x


_LAYER_PARAMS = ("attn_norm", "w_in", "qk_conv_w", "qk_conv_b", "i_bias", "f_bias", "mlstm_norm",
                 "cmp_pe_k", "cmp_pe_v", "cmp_w1_k", "cmp_w2_k", "cmp_w1_v", "cmp_w2_v", "nsa_norm",
                 "w_out", "ffn_norm", "w_up", "ffn_conv_w", "ffn_conv_b", "w_down")


def kernel(x, attn_norm, w_in, qk_conv_w, qk_conv_b, i_bias, f_bias, mlstm_norm, cmp_pe_k, cmp_pe_v,
           cmp_w1_k, cmp_w2_k, cmp_w1_v, cmp_w2_v, nsa_norm, w_out, ffn_norm, w_up, ffn_conv_w,
           ffn_conv_b, w_down, final_norm):
    stacked = dict(zip(_LAYER_PARAMS, (attn_norm, w_in, qk_conv_w, qk_conv_b, i_bias, f_bias,
                                       mlstm_norm, cmp_pe_k, cmp_pe_v, cmp_w1_k, cmp_w2_k, cmp_w1_v,
                                       cmp_w2_v, nsa_norm, w_out, ffn_norm, w_up, ffn_conv_w,
                                       ffn_conv_b, w_down)))
    batch, seq, d = x.shape
    assert d == D_MODEL and seq % MM_TM == 0 and seq % ATTN_TQ == 0 and seq >= WIN_KEYS
    assert seq // CMP_STRIDE == LANES and seq // SEL_BLOCK <= LANES
    cosf, sinf = _rope_tables(seq)
    overlap = _overlap_matrix(seq, seq // CMP_STRIDE)
    onehot = _block_onehot(seq)
    xf = x.reshape(batch * seq, d)
    for layer in range(DEPTH):
        p = {k: v[layer] for k, v in stacked.items()}
        xf = _layer(xf, p, cosf, sinf, overlap, onehot, batch, seq)
    return rmsnorm(xf, final_norm).reshape(batch, seq, d)
```

```python
import functools

import jax
import jax.numpy as jnp
from jax import lax
from jax.experimental import pallas as pl
from jax.experimental.pallas import tpu as pltpu

D_MODEL = 2048
DEPTH = 2
MLSTM_HEADS = 8
HEAD_DIM = 128
MLSTM_W = MLSTM_HEADS * HEAD_DIM
QK_CONV = 4
NSA_HEADS = 8
NSA_KV_HEADS = 2
NSA_GROUP = NSA_HEADS // NSA_KV_HEADS
NSA_W = NSA_HEADS * HEAD_DIM
NSA_KV_W = NSA_KV_HEADS * HEAD_DIM
CMP_BLOCK = 32
CMP_STRIDE = 16
CMP_HIDDEN = 256
SEL_BLOCK = 64
SEL_TOPK = 16
WINDOW = 512
ROPE_THETA = 500000.0
ROPE_DIM = HEAD_DIM // 4
D_FF = 5632
FFN_CONV = 3
NORM_EPS = 1e-6
NEG = -1e30
FORCE_SCORE = 1e4
ATTN_SCALE = HEAD_DIM ** -0.5

LANES = 128
SUBLANES_F32 = 8
SUBLANES_BF16 = 16
VMEM_LIMIT_BYTES = 48 * 1024 * 1024

MM_TM = 1024
MM_TN = 512
EPI_ROWS = 256
DOWN_TN = 256
MLSTM_CHUNK = 128
MLSTM_HEADS_PER_STEP = 4
ATTN_TQ = 256
ATTN_TK = 256
WIN_KEYS = WINDOW + ATTN_TQ
ROW_TM = 512
CONV_HALO = SUBLANES_BF16

BF16 = jnp.bfloat16
F32 = jnp.float32

W_IN_QK = (0, 2 * MLSTM_W)
W_IN_VO = (2 * MLSTM_W, 4 * MLSTM_W)
W_IN_MI = 4 * MLSTM_W
W_IN_NSA = (W_IN_MI + 2 * MLSTM_HEADS, W_IN_MI + 2 * MLSTM_HEADS + NSA_W + 6 * NSA_KV_W)
W_IN_NG = W_IN_NSA[1]
ZN_W = NSA_W + 6 * NSA_KV_W
ZN_KC, ZN_VC, ZN_KS, ZN_VS, ZN_KW, ZN_VW = (NSA_W + i * NSA_KV_W for i in range(6))
ZG_MI, ZG_MF, ZG_NG = 0, MLSTM_HEADS, 2 * MLSTM_HEADS


def _cparams(*sem):
    return pltpu.CompilerParams(dimension_semantics=sem, vmem_limit_bytes=VMEM_LIMIT_BYTES)


def _rms(x, gain):
    return x * lax.rsqrt(jnp.mean(x * x, axis=-1, keepdims=True) + NORM_EPS) * gain


def _nt_dot(a, b):
    return lax.dot_general(a, b, (((1,), (1,)), ((), ())), preferred_element_type=F32)


def _stack_heads(q_ref):
    return jnp.concatenate([q_ref[:, g * HEAD_DIM:(g + 1) * HEAD_DIM] for g in range(NSA_GROUP)], axis=0)


def _norm_matmul_kernel(x_ref, g_ref, w_ref, o_ref, h_ref):
    @pl.when(pl.program_id(1) == 0)
    def _():
        h_ref[...] = _rms(x_ref[...], g_ref[...]).astype(h_ref.dtype)

    o_ref[...] = jnp.dot(h_ref[...], w_ref[...], preferred_element_type=F32).astype(o_ref.dtype)


def norm_matmul(x, g, w, out_dtype):
    m, k = x.shape
    n = w.shape[1]
    tm, tn = MM_TM, MM_TN
    return pl.pallas_call(
        _norm_matmul_kernel,
        out_shape=jax.ShapeDtypeStruct((m, n), out_dtype),
        grid=(m // tm, n // tn),
        in_specs=[
            pl.BlockSpec((tm, k), lambda i, j: (i, 0)),
            pl.BlockSpec((1, k), lambda i, j: (0, 0)),
            pl.BlockSpec((k, tn), lambda i, j: (0, j)),
        ],
        out_specs=pl.BlockSpec((tm, tn), lambda i, j: (i, j)),
        scratch_shapes=[pltpu.VMEM((tm, k), BF16)],
        compiler_params=_cparams("parallel", "arbitrary"),
        name="norm_matmul",
    )(x, g.reshape(1, k), w)


def _fill_halo_h(tiles_per_seq, x_ref, xp_ref, g_ref, h_s):
    tm = x_ref.shape[0]
    first = pl.program_id(0) % tiles_per_seq == 0
    hp = _rms(xp_ref[...], g_ref[...])
    h_s[0:CONV_HALO, :] = jnp.where(first, 0.0, hp).astype(h_s.dtype)
    h_s[CONV_HALO:CONV_HALO + tm, :] = _rms(x_ref[...], g_ref[...]).astype(h_s.dtype)


def _causal_conv(u_s, cw_ref, cb_ref, taps, tm):
    acc = jnp.broadcast_to(cb_ref[...], (tm, u_s.shape[1]))
    for j in range(taps):
        off = CONV_HALO - (taps - 1) + j
        acc = acc + u_s[off:off + tm, :] * cw_ref[j:j + 1, :]
    return acc


def _proj_qk_kernel(tiles_per_seq, n_q_tiles, x_ref, xp_ref, g_ref, w_ref, wg_ref, cw_ref, cb_ref,
                    o_ref, zg_ref, h_s, u_s):
    tm = x_ref.shape[0]
    j = pl.program_id(1)

    @pl.when(j == 0)
    def _():
        _fill_halo_h(tiles_per_seq, x_ref, xp_ref, g_ref, h_s)
        zg_ref[...] = jnp.dot(h_s[CONV_HALO:CONV_HALO + tm, :], wg_ref[...], preferred_element_type=F32)

    u_s[...] = jnp.dot(h_s[...], w_ref[...], preferred_element_type=F32)
    y = _causal_conv(u_s, cw_ref, cb_ref, QK_CONV, tm)
    y = y * jax.nn.sigmoid(y)
    scale = jnp.where(j >= n_q_tiles, ATTN_SCALE, 1.0)
    o_ref[...] = (y * scale).astype(o_ref.dtype)


def proj_qk(x, g, w_qk, w_gate, conv_w, conv_b, seq):
    m, k = x.shape
    tm, tn, halo = MM_TM, MM_TN, CONV_HALO
    n = w_qk.shape[1]
    return pl.pallas_call(
        functools.partial(_proj_qk_kernel, seq // tm, MLSTM_W // tn),
        out_shape=(jax.ShapeDtypeStruct((m, n), BF16), jax.ShapeDtypeStruct((m, LANES), F32)),
        grid=(m // tm, n // tn),
        in_specs=[
            pl.BlockSpec((tm, k), lambda i, j: (i, 0)),
            pl.BlockSpec((halo, k), lambda i, j: (jnp.maximum(i * (tm // halo) - 1, 0), 0)),
            pl.BlockSpec((1, k), lambda i, j: (0, 0)),
            pl.BlockSpec((k, tn), lambda i, j: (0, j)),
            pl.BlockSpec((k, LANES), lambda i, j: (0, 0)),
            pl.BlockSpec((QK_CONV, tn), lambda i, j: (0, j)),
            pl.BlockSpec((1, tn), lambda i, j: (0, j)),
        ],
        out_specs=(pl.BlockSpec((tm, tn), lambda i, j: (i, j)),
                   pl.BlockSpec((tm, LANES), lambda i, j: (i, 0))),
        scratch_shapes=[pltpu.VMEM((tm + halo, k), BF16), pltpu.VMEM((tm + halo, tn), F32)],
        compiler_params=_cparams("parallel", "arbitrary"),
        name="proj_qk",
    )(x, x, g.reshape(1, k), w_qk, w_gate, conv_w, conv_b.reshape(1, -1))


def _rope(x, cosf, sinf, lane):
    rot = jnp.where(lane < ROPE_DIM // 2, pltpu.roll(x, LANES - ROPE_DIM // 2, axis=1),
                    pltpu.roll(x, ROPE_DIM // 2, axis=1))
    return x * cosf + rot * sinf


def _proj_nsa_kernel(n_q_tiles, x_ref, g_ref, w_ref, cos_ref, sin_ref, o_ref, h_s):
    j = pl.program_id(1)

    @pl.when(j == 0)
    def _():
        h_s[...] = _rms(x_ref[...], g_ref[...]).astype(h_s.dtype)

    is_q = j < n_q_tiles
    tm = x_ref.shape[0]
    for r in range(tm // EPI_ROWS):
        rs = slice(r * EPI_ROWS, (r + 1) * EPI_ROWS)
        cosf, sinf = cos_ref[rs, :], sin_ref[rs, :]
        lane = lax.broadcasted_iota(jnp.int32, cosf.shape, 1)
        u = jnp.dot(h_s[rs, :], w_ref[...], preferred_element_type=F32)
        for head in range(o_ref.shape[1] // HEAD_DIM):
            cs = slice(head * HEAD_DIM, (head + 1) * HEAD_DIM)
            xh = u[:, cs]
            roped = _rope(xh, cosf, sinf, lane)
            if head < NSA_KV_HEADS:
                yh = roped * jnp.where(is_q, ATTN_SCALE, 1.0)
            else:
                yh = jnp.where(is_q, roped * ATTN_SCALE, xh)
            o_ref[rs, cs] = yh.astype(o_ref.dtype)


def proj_nsa(x, g, w_nsa, cosf, sinf, seq):
    m, k = x.shape
    tm, tn = MM_TM, MM_TN
    n = w_nsa.shape[1]
    assert tn == 2 * NSA_KV_W
    nper = seq // tm
    return pl.pallas_call(
        functools.partial(_proj_nsa_kernel, NSA_W // tn),
        out_shape=jax.ShapeDtypeStruct((m, n), BF16),
        grid=(m // tm, n // tn),
        in_specs=[
            pl.BlockSpec((tm, k), lambda i, j: (i, 0)),
            pl.BlockSpec((1, k), lambda i, j: (0, 0)),
            pl.BlockSpec((k, tn), lambda i, j: (0, j)),
            pl.BlockSpec((tm, LANES), lambda i, j: (i % nper, 0)),
            pl.BlockSpec((tm, LANES), lambda i, j: (i % nper, 0)),
        ],
        out_specs=pl.BlockSpec((tm, tn), lambda i, j: (i, j)),
        scratch_shapes=[pltpu.VMEM((tm, k), BF16)],
        compiler_params=_cparams("parallel", "arbitrary"),
        name="proj_nsa",
    )(x, g.reshape(1, k), w_nsa, cosf, sinf)


def _matmul_res_kernel(n_lhs, *refs):
    lhs = refs[:n_lhs]
    ws = refs[n_lhs:2 * n_lhs]
    res_ref, o_ref = refs[2 * n_lhs], refs[2 * n_lhs + 1]
    acc = res_ref[...]
    for a_ref, w_ref in zip(lhs, ws):
        acc = acc + jnp.dot(a_ref[...], w_ref[...], preferred_element_type=F32)
    o_ref[...] = acc


def matmul_res(lhs_list, w_list, res, tn):
    m, n = res.shape
    n_lhs = len(lhs_list)
    in_specs = [pl.BlockSpec((MM_TM, a.shape[1]), lambda i, j: (i, 0)) for a in lhs_list]
    in_specs += [pl.BlockSpec((w.shape[0], tn), lambda i, j: (0, j)) for w in w_list]
    in_specs += [pl.BlockSpec((MM_TM, tn), lambda i, j: (i, j))]
    return pl.pallas_call(
        functools.partial(_matmul_res_kernel, n_lhs),
        out_shape=jax.ShapeDtypeStruct((m, n), F32),
        grid=(m // MM_TM, n // tn),
        in_specs=in_specs,
        out_specs=pl.BlockSpec((MM_TM, tn), lambda i, j: (i, j)),
        compiler_params=_cparams("parallel", "arbitrary"),
        name="matmul_res",
    )(*lhs_list, *w_list, res)


def _mlstm_kernel(seq, q_ref, k_ref, v_ref, zo_ref, gi_ref, gf_ref, bi_ref, bf_ref, nrm_ref, o_ref,
                  b_s, i_s, cx_s, m_s):
    L = MLSTM_CHUNK
    hps = MLSTM_HEADS_PER_STEP
    nchunks = seq // L

    lane = lax.broadcasted_iota(jnp.int32, (nchunks, L), 1)
    for hh in range(hps):
        i_s[hh] = gi_ref[0, hh] + bi_ref[hh]
        fpre = gf_ref[0, hh] + bf_ref[hh]
        csum = jnp.minimum(fpre, 0.0) - jnp.log1p(jnp.exp(-jnp.abs(fpre)))
        for sh in (1, 2, 4, 8, 16, 32, 64):
            csum = csum + jnp.where(lane >= sh, pltpu.roll(csum, sh, axis=1), 0.0)
        b_s[hh] = csum

    cx_s[...] = jnp.zeros_like(cx_s)
    m_s[...] = jnp.zeros_like(m_s)

    row = lax.broadcasted_iota(jnp.int32, (L, L), 0)
    col = lax.broadcasted_iota(jnp.int32, (L, L), 1)
    eye = row == col
    causal = col <= row
    ones = jnp.ones((L, LANES), BF16)

    def chunk_body(c, carry):
        r0 = pl.multiple_of(c * L, L)
        for hh in range(hps):
            cs = slice(hh * HEAD_DIM, (hh + 1) * HEAD_DIM)
            qc = q_ref[pl.ds(r0, L), cs]
            kc = k_ref[pl.ds(r0, L), cs]
            vx = jnp.concatenate([v_ref[pl.ds(r0, L), cs], ones], axis=1)
            b_row = b_s[hh, pl.ds(c, 1), :]
            u_row = i_s[hh, pl.ds(c, 1), :] - b_row
            b_col = jnp.sum(jnp.where(eye, b_row, 0.0), axis=-1, keepdims=True)
            m_prev = m_s[hh]
            cx = cx_s[hh]

            umat = jnp.where(causal, u_row, -jnp.inf)
            m_col = jnp.maximum(m_prev, jnp.max(umat, axis=-1, keepdims=True))
            w_inter = jnp.exp(m_prev - m_col)
            s_qk = _nt_dot(qc, kc) * jnp.exp(umat - m_col)
            nd = (w_inter * jnp.dot(qc, cx.astype(BF16), preferred_element_type=F32)
                  + jnp.dot(s_qk.astype(BF16), vx, preferred_element_type=F32))
            floor = jnp.exp(-(b_col + m_col))
            h = nd[:, :HEAD_DIM] / jnp.maximum(jnp.abs(nd[:, HEAD_DIM:]), floor)

            m_last = m_col[L - 1:L, :]
            kw_t = kc.astype(F32).T * jnp.exp(u_row - m_last)
            cx_s[hh] = (jnp.exp(m_prev - m_last) * cx
                        + jnp.dot(kw_t.astype(BF16), vx, preferred_element_type=F32))
            m_s[hh] = b_row[:, L - 1:L] + m_last

            hn = _rms(h, nrm_ref[:, cs])
            gate = jax.nn.sigmoid(zo_ref[pl.ds(r0, L), cs].astype(F32))
            o_ref[pl.ds(r0, L), cs] = (hn * gate).astype(o_ref.dtype)
        return carry

    lax.fori_loop(0, nchunks, chunk_body, 0)


def mlstm_group(zqk, zvo, gi, gf, bi, bf, norm_g, batch, seq):
    hps = MLSTM_HEADS_PER_STEP
    wblk = hps * HEAD_DIM
    nchunks = seq // MLSTM_CHUNK
    nhb = MLSTM_HEADS // hps
    lo = pl.BlockSpec((seq, wblk), lambda b, h: (b, h))
    hi = pl.BlockSpec((seq, wblk), lambda b, h: (b, nhb + h))
    gspec = pl.BlockSpec((1, hps, nchunks, MLSTM_CHUNK), lambda b, h: (b, h, 0, 0))
    bspec = pl.BlockSpec((hps, 1, MLSTM_CHUNK), lambda b, h: (h, 0, 0))
    return pl.pallas_call(
        functools.partial(_mlstm_kernel, seq),
        out_shape=jax.ShapeDtypeStruct((batch * seq, MLSTM_W), BF16),
        grid=(batch, nhb),
        in_specs=[lo, hi, lo, hi, gspec, gspec, bspec, bspec,
                  pl.BlockSpec((1, wblk), lambda b, h: (0, h))],
        out_specs=pl.BlockSpec((seq, wblk), lambda b, h: (b, h)),
        scratch_shapes=[
            pltpu.VMEM((hps, nchunks, MLSTM_CHUNK), F32),
            pltpu.VMEM((hps, nchunks, MLSTM_CHUNK), F32),
            pltpu.VMEM((hps, HEAD_DIM, 2 * HEAD_DIM), F32),
            pltpu.VMEM((hps, 1, 1), F32),
        ],
        compiler_params=_cparams("parallel", "parallel"),
        name="mlstm_group",
    )(zqk, zqk, zvo, zvo, gi, gf, bi, bf, norm_g.reshape(1, -1))


def _compress_kernel(x_ref, pe_ref, w1_ref, w2_ref, o_ref, sh_s):
    half = CMP_STRIDE * HEAD_DIM
    ngroups = x_ref.shape[1]
    x = x_ref[0].astype(F32)
    lo = jnp.dot((x + pe_ref[:, :half]).astype(BF16), w1_ref[:half, :], preferred_element_type=F32)
    hi = jnp.dot((x + pe_ref[:, half:]).astype(BF16), w1_ref[half:, :], preferred_element_type=F32)
    sh_s[0:ngroups, :] = hi
    sh_s[ngroups:ngroups + SUBLANES_F32, :] = jnp.zeros((SUBLANES_F32, CMP_HIDDEN), F32)
    pre = lo + sh_s[1:ngroups + 1, :]
    act = jax.nn.gelu(pre, approximate=True)
    o_ref[0] = jnp.dot(act.astype(BF16), w2_ref[...], preferred_element_type=F32).astype(o_ref.dtype)


def compress(xg, pe, w1, w2):
    nslab, ngroups, gw = xg.shape
    return pl.pallas_call(
        _compress_kernel,
        out_shape=jax.ShapeDtypeStruct((nslab, ngroups, HEAD_DIM), BF16),
        grid=(nslab,),
        in_specs=[pl.BlockSpec((1, ngroups, gw), lambda s: (s, 0, 0)),
                  pl.BlockSpec((1, 2 * gw), lambda s: (0, 0)),
                  pl.BlockSpec((2 * gw, CMP_HIDDEN), lambda s: (0, 0)),
                  pl.BlockSpec((CMP_HIDDEN, HEAD_DIM), lambda s: (0, 0))],
        out_specs=pl.BlockSpec((1, ngroups, HEAD_DIM), lambda s: (s, 0, 0)),
        scratch_shapes=[pltpu.VMEM((ngroups + SUBLANES_F32, CMP_HIDDEN), F32)],
        compiler_params=_cparams("parallel"),
        name="nsa_compress",
    )(xg, pe.reshape(1, -1), w1, w2)


def _cmp_select_kernel(nsel, q_ref, kc_ref, vc_ref, ov_ref, o_ref, sel_ref):
    tq = ATTN_TQ
    s0 = pl.program_id(2) * tq
    kc = kc_ref[0]
    vc = vc_ref[0]
    ncp = kc.shape[0]
    pos = s0 + lax.broadcasted_iota(jnp.int32, (tq, ncp), 0)
    cidx = lax.broadcasted_iota(jnp.int32, (tq, ncp), 1)
    valid = cidx * CMP_STRIDE + (CMP_BLOCK - 1) <= pos
    any_valid = s0 + lax.broadcasted_iota(jnp.int32, (tq, 1), 0) >= CMP_BLOCK - 1
    sc = _nt_dot(_stack_heads(q_ref), kc)
    sc = jnp.where(valid[None], sc.reshape(NSA_GROUP, tq, ncp), NEG)
    e = jnp.where(valid[None], jnp.exp(sc - jnp.max(sc, axis=-1, keepdims=True)), 0.0)
    denom = jnp.sum(e, axis=-1, keepdims=True)
    p = e / jnp.where(any_valid[None], denom, 1.0)
    o = jnp.dot(p.reshape(NSA_GROUP * tq, ncp).astype(BF16), vc, preferred_element_type=F32)
    p_sum = p[0]
    for g in range(NSA_GROUP):
        o_ref[:, g * HEAD_DIM:(g + 1) * HEAD_DIM] = o[g * tq:(g + 1) * tq, :].astype(o_ref.dtype)
        if g:
            p_sum = p_sum + p[g]
    imp = jnp.dot(p_sum, ov_ref[...], preferred_element_type=F32, precision=lax.Precision.HIGHEST)
    jb = cidx
    cur = pos // SEL_BLOCK
    forced = (jb == 0) | (jb == cur) | (jb == cur - 1)
    imp = jnp.where(forced, FORCE_SCORE, imp)
    imp = jnp.where(jb <= cur, imp, -1.0)
    imp_t = imp.T[:nsel, :]
    jb_t = lax.broadcasted_iota(jnp.int32, (nsel, tq), 0)
    rank = jnp.zeros((nsel, tq), F32)
    for j2 in range(nsel):
        rowv = imp_t[j2:j2 + 1, :]
        before = (rowv > imp_t) | ((rowv == imp_t) & (jb_t > j2))
        rank = rank + jnp.where(before, 1.0, 0.0)
    sel_t = jnp.where(rank < float(SEL_TOPK), 1.0, 0.0)
    sel_t = jnp.concatenate([sel_t, jnp.zeros((ncp - nsel, tq), F32)], axis=0)
    sel_ref[...] = sel_t.T.astype(sel_ref.dtype)


def cmp_select(zn, kcmp, vcmp, overlap, batch, seq):
    m = zn.shape[0]
    tq = ATTN_TQ
    nq = seq // tq
    ncp = kcmp.shape[1]
    gw = NSA_GROUP * HEAD_DIM
    return pl.pallas_call(
        functools.partial(_cmp_select_kernel, seq // SEL_BLOCK),
        out_shape=(jax.ShapeDtypeStruct((m, NSA_W), BF16),
                   jax.ShapeDtypeStruct((m, NSA_KV_HEADS * LANES), BF16)),
        grid=(batch, NSA_KV_HEADS, nq),
        in_specs=[pl.BlockSpec((tq, gw), lambda b, h, i: (b * nq + i, h)),
                  pl.BlockSpec((1, ncp, HEAD_DIM), lambda b, h, i: (b * NSA_KV_HEADS + h, 0, 0)),
                  pl.BlockSpec((1, ncp, HEAD_DIM), lambda b, h, i: (b * NSA_KV_HEADS + h, 0, 0)),
                  pl.BlockSpec((ncp, LANES), lambda b, h, i: (0, 0))],
        out_specs=(pl.BlockSpec((tq, gw), lambda b, h, i: (b * nq + i, h)),
                   pl.BlockSpec((tq, LANES), lambda b, h, i: (b * nq + i, h))),
        compiler_params=_cparams("parallel", "parallel", "parallel"),
        name="nsa_cmp_select",
    )(zn, kcmp, vcmp, overlap)


def _sel_attn_kernel(q_ref, k_ref, v_ref, oh_ref, sel_ref, o_ref, qx_s, m_s, acc_s, s0_s, s1_s):
    tq, tk = ATTN_TQ, ATTN_TK
    i = pl.program_id(2)
    penalty = ((sel_ref[...].astype(F32) - 1.0) * (-NEG)).astype(BF16)
    for g in range(NSA_GROUP):
        cs = slice(g * HEAD_DIM, (g + 1) * HEAD_DIM)
        qx_s[g * tq:(g + 1) * tq, :] = jnp.concatenate([q_ref[:, cs], penalty], axis=1)
    m_s[...] = jnp.full_like(m_s, NEG)
    acc_s[...] = jnp.zeros_like(acc_s)
    ones = jnp.ones((tk, LANES), BF16)

    def scores(t, s_buf):
        k0 = pl.multiple_of(t * tk, tk)
        kx = jnp.concatenate([k_ref[pl.ds(k0, tk), :], oh_ref[pl.ds(k0, tk), :]], axis=1)
        s_buf[...] = _nt_dot(qx_s[...], kx)

    def update(t, s_buf, diagonal):
        k0 = pl.multiple_of(t * tk, tk)
        vx = jnp.concatenate([v_ref[pl.ds(k0, tk), :], ones], axis=1)
        s = s_buf[...]
        if diagonal:
            row = lax.broadcasted_iota(jnp.int32, (tq, tk), 0)
            col = lax.broadcasted_iota(jnp.int32, (tq, tk), 1)
            causal = jnp.where(col <= row, 0.0, NEG)
            s = (s.reshape(NSA_GROUP, tq, tk) + causal[None]).reshape(NSA_GROUP * tq, tk)
        m_old = m_s[...]
        m_new = jnp.maximum(m_old, jnp.max(s, axis=-1, keepdims=True))
        alpha = jnp.exp(m_old - m_new)
        p = jnp.exp((s - jnp.concatenate([m_new] * (tk // LANES), axis=1)).astype(BF16))
        acc_s[...] = (jnp.concatenate([alpha, alpha], axis=1) * acc_s[...]
                      + jnp.dot(p, vx, preferred_element_type=F32))
        m_s[...] = m_new

    scores(0, s0_s)

    def pair(p, carry):
        t = 2 * p
        scores(t + 1, s1_s)
        update(t, s0_s, False)
        scores(t + 2, s0_s)
        update(t + 1, s1_s, False)
        return carry

    lax.fori_loop(0, i // 2, pair, 0)

    @pl.when(i % 2 == 0)
    def _():
        update(i, s0_s, True)

    @pl.when(i % 2 == 1)
    def _():
        scores(i, s1_s)
        update(i - 1, s0_s, False)
        update(i, s1_s, True)
    for g in range(NSA_GROUP):
        cs = slice(g * HEAD_DIM, (g + 1) * HEAD_DIM)
        acc = acc_s[g * tq:(g + 1) * tq, :]
        o_ref[:, cs] = (acc[:, :HEAD_DIM] / acc[:, HEAD_DIM:]).astype(o_ref.dtype)


def _kv_spec(seq, col0):
    return pl.BlockSpec((seq, HEAD_DIM), lambda b, h, i: (b, col0 // HEAD_DIM + h))


def sel_attn(zn, onehot, sel, batch, seq):
    assert ATTN_TQ == ATTN_TK
    m = zn.shape[0]
    tq = ATTN_TQ
    nq = seq // tq
    gw = NSA_GROUP * HEAD_DIM
    return pl.pallas_call(
        _sel_attn_kernel,
        out_shape=jax.ShapeDtypeStruct((m, NSA_W), BF16),
        grid=(batch, NSA_KV_HEADS, nq),
        in_specs=[pl.BlockSpec((tq, gw), lambda b, h, i: (b * nq + i, h)),
                  _kv_spec(seq, ZN_KS), _kv_spec(seq, ZN_VS),
                  pl.BlockSpec((seq, LANES), lambda b, h, i: (0, 0)),
                  pl.BlockSpec((tq, LANES), lambda b, h, i: (b * nq + i, h))],
        out_specs=pl.BlockSpec((tq, gw), lambda b, h, i: (b * nq + i, h)),
        scratch_shapes=[pltpu.VMEM((NSA_GROUP * tq, 2 * LANES), BF16),
                        pltpu.VMEM((NSA_GROUP * tq, LANES), F32),
                        pltpu.VMEM((NSA_GROUP * tq, 2 * LANES), F32),
                        pltpu.VMEM((NSA_GROUP * tq, ATTN_TK), F32),
                        pltpu.VMEM((NSA_GROUP * tq, ATTN_TK), F32)],
        compiler_params=_cparams("parallel", "parallel", "arbitrary"),
        name="nsa_sel_attn",
    )(zn, zn, zn, onehot, sel)


def _win_attn_kernel(q_ref, k_ref, v_ref, o_ref):
    tq, nk = ATTN_TQ, WIN_KEYS
    s0 = pl.program_id(2) * tq
    start = pl.multiple_of(jnp.maximum(s0 - WINDOW, 0), tq)
    kw = k_ref[pl.ds(start, nk), :]
    vx = jnp.concatenate([v_ref[pl.ds(start, nk), :], jnp.ones((nk, LANES), BF16)], axis=1)
    qpos = s0 + lax.broadcasted_iota(jnp.int32, (tq, nk), 0)
    kpos = start + lax.broadcasted_iota(jnp.int32, (tq, nk), 1)
    diff = qpos - kpos
    bias = jnp.where((diff >= 0) & (diff < WINDOW), 0.0, NEG)
    for g in range(NSA_GROUP):
        cs = slice(g * HEAD_DIM, (g + 1) * HEAD_DIM)
        s = _nt_dot(q_ref[:, cs], kw) + bias
        p = jnp.exp((s - jnp.max(s, axis=-1, keepdims=True)).astype(BF16))
        a = jnp.dot(p, vx, preferred_element_type=F32)
        o_ref[:, cs] = (a[:, :HEAD_DIM] / a[:, HEAD_DIM:]).astype(o_ref.dtype)


def win_attn(zn, batch, seq):
    m = zn.shape[0]
    tq = ATTN_TQ
    nq = seq // tq
    gw = NSA_GROUP * HEAD_DIM
    return pl.pallas_call(
        _win_attn_kernel,
        out_shape=jax.ShapeDtypeStruct((m, NSA_W), BF16),
        grid=(batch, NSA_KV_HEADS, nq),
        in_specs=[pl.BlockSpec((tq, gw), lambda b, h, i: (b * nq + i, h)),
                  _kv_spec(seq, ZN_KW), _kv_spec(seq, ZN_VW)],
        out_specs=pl.BlockSpec((tq, gw), lambda b, h, i: (b * nq + i, h)),
        compiler_params=_cparams("parallel", "parallel", "parallel"),
        name="nsa_win_attn",
    )(zn, zn, zn)


def _nsa_combine_kernel(oc_ref, os_ref, ow_ref, zg_ref, nrm_ref, o_ref, mix_s):
    tm = oc_ref.shape[0]
    gates = jax.nn.sigmoid(zg_ref[...])
    ssq = jnp.zeros((tm, 1), F32)
    for h in range(NSA_HEADS):
        cs = slice(h * HEAD_DIM, (h + 1) * HEAD_DIM)
        c0 = ZG_NG + 3 * h
        mix = (gates[:, c0:c0 + 1] * oc_ref[:, cs].astype(F32)
               + gates[:, c0 + 1:c0 + 2] * os_ref[:, cs].astype(F32)
               + gates[:, c0 + 2:c0 + 3] * ow_ref[:, cs].astype(F32))
        mix_s[:, cs] = mix
        ssq = ssq + jnp.sum(mix * mix, axis=-1, keepdims=True)
    inv = lax.rsqrt(ssq / NSA_W + NORM_EPS)
    o_ref[...] = (mix_s[...] * inv * nrm_ref[...]).astype(o_ref.dtype)


def nsa_combine(o_cmp, o_slc, o_win, zg, norm_g):
    m = o_cmp.shape[0]
    tm = ROW_TM
    ospec = pl.BlockSpec((tm, NSA_W), lambda i: (i, 0))
    return pl.pallas_call(
        _nsa_combine_kernel,
        out_shape=jax.ShapeDtypeStruct((m, NSA_W), BF16),
        grid=(m // tm,),
        in_specs=[ospec, ospec, ospec, pl.BlockSpec((tm, LANES), lambda i: (i, 0)),
                  pl.BlockSpec((1, NSA_W), lambda i: (0, 0))],
        out_specs=ospec,
        scratch_shapes=[pltpu.VMEM((tm, NSA_W), F32)],
        compiler_params=_cparams("parallel"),
        name="nsa_combine",
    )(o_cmp, o_slc, o_win, zg, norm_g.reshape(1, -1))


def _ffn_up_kernel(tiles_per_seq, x_ref, xp_ref, g_ref, wg_ref, wu_ref, cwg_ref, cwu_ref,
                   cbg_ref, cbu_ref, o_ref, h_s, ug_s, uu_s):
    tm = x_ref.shape[0]

    @pl.when(pl.program_id(1) == 0)
    def _():
        _fill_halo_h(tiles_per_seq, x_ref, xp_ref, g_ref, h_s)

    ug_s[...] = jnp.dot(h_s[...], wg_ref[...], preferred_element_type=F32)
    uu_s[...] = jnp.dot(h_s[...], wu_ref[...], preferred_element_type=F32)
    gate = _causal_conv(ug_s, cwg_ref, cbg_ref, FFN_CONV, tm)
    up = _causal_conv(uu_s, cwu_ref, cbu_ref, FFN_CONV, tm)
    o_ref[...] = (gate * jax.nn.sigmoid(gate) * up).astype(o_ref.dtype)


def ffn_up(x, g, w_up, conv_w, conv_b, seq):
    m, k = x.shape
    tm, tn, halo = MM_TM, MM_TN, CONV_HALO
    ntile = D_FF // tn
    cb = conv_b.reshape(1, -1)
    return pl.pallas_call(
        functools.partial(_ffn_up_kernel, seq // tm),
        out_shape=jax.ShapeDtypeStruct((m, D_FF), BF16),
        grid=(m // tm, ntile),
        in_specs=[
            pl.BlockSpec((tm, k), lambda i, j: (i, 0)),
            pl.BlockSpec((halo, k), lambda i, j: (jnp.maximum(i * (tm // halo) - 1, 0), 0)),
            pl.BlockSpec((1, k), lambda i, j: (0, 0)),
            pl.BlockSpec((k, tn), lambda i, j: (0, j)),
            pl.BlockSpec((k, tn), lambda i, j: (0, ntile + j)),
            pl.BlockSpec((FFN_CONV, tn), lambda i, j: (0, j)),
            pl.BlockSpec((FFN_CONV, tn), lambda i, j: (0, ntile + j)),
            pl.BlockSpec((1, tn), lambda i, j: (0, j)),
            pl.BlockSpec((1, tn), lambda i, j: (0, ntile + j)),
        ],
        out_specs=pl.BlockSpec((tm, tn), lambda i, j: (i, j)),
        scratch_shapes=[pltpu.VMEM((tm + halo, k), BF16),
                        pltpu.VMEM((tm + halo, tn), F32),
                        pltpu.VMEM((tm + halo, tn), F32)],
        compiler_params=_cparams("parallel", "arbitrary"),
        name="ffn_up",
    )(x, x, g.reshape(1, k), w_up, w_up, conv_w, conv_w, cb, cb)


def _rmsnorm_kernel(x_ref, g_ref, o_ref):
    o_ref[...] = _rms(x_ref[...], g_ref[...])


def rmsnorm(x, g):
    m, k = x.shape
    tm = ROW_TM
    return pl.pallas_call(
        _rmsnorm_kernel,
        out_shape=jax.ShapeDtypeStruct((m, k), F32),
        grid=(m // tm,),
        in_specs=[pl.BlockSpec((tm, k), lambda i: (i, 0)), pl.BlockSpec((1, k), lambda i: (0, 0))],
        out_specs=pl.BlockSpec((tm, k), lambda i: (i, 0)),
        compiler_params=_cparams("parallel"),
        name="final_rmsnorm",
    )(x, g.reshape(1, k))


def _gate_weight(w_in):
    gates = jnp.concatenate([w_in[:, W_IN_MI:W_IN_MI + 2 * MLSTM_HEADS],
                             w_in[:, W_IN_NG:W_IN_NG + 3 * NSA_HEADS]], axis=1)
    return jnp.pad(gates, ((0, 0), (0, LANES - gates.shape[1]))).astype(BF16)


def _rope_tables(seq):
    pos = jnp.arange(seq, dtype=F32)
    inv = ROPE_THETA ** (-jnp.arange(0, ROPE_DIM, 2, dtype=F32) / ROPE_DIM)
    ang = pos[:, None] * inv[None, :]
    cos, sin = jnp.cos(ang), jnp.sin(ang)
    rest = HEAD_DIM - ROPE_DIM
    cosf = jnp.concatenate([cos, cos, jnp.ones((seq, rest), F32)], axis=1)
    sinf = jnp.concatenate([-sin, sin, jnp.zeros((seq, rest), F32)], axis=1)
    return cosf, sinf


def _overlap_matrix(seq, ncmp_pad):
    nsel = seq // SEL_BLOCK
    cstart = jnp.arange(ncmp_pad) * CMP_STRIDE
    jb = jnp.arange(LANES)
    ov = ((cstart[:, None] < (jb[None, :] + 1) * SEL_BLOCK)
          & (cstart[:, None] + CMP_BLOCK > jb[None, :] * SEL_BLOCK) & (jb[None, :] < nsel))
    return ov.astype(F32)


def _block_onehot(seq):
    blk = jnp.arange(seq)[:, None] // SEL_BLOCK
    return (blk == jnp.arange(LANES)[None, :]).astype(BF16)


def _to_groups(a, batch, seq):
    a = a.reshape(batch, seq, NSA_KV_HEADS, HEAD_DIM).transpose(0, 2, 1, 3)
    return a.reshape(batch * NSA_KV_HEADS, seq // CMP_STRIDE, CMP_STRIDE * HEAD_DIM)


def _layer(x, p, cosf, sinf, overlap, onehot, batch, seq):
    w_in = p["w_in"]
    w_qk = w_in[:, W_IN_QK[0]:W_IN_QK[1]].astype(BF16)
    w_vo = w_in[:, W_IN_VO[0]:W_IN_VO[1]].astype(BF16)
    w_nsa = w_in[:, W_IN_NSA[0]:W_IN_NSA[1]].astype(BF16)
    zqk, zg = proj_qk(x, p["attn_norm"], w_qk, _gate_weight(w_in), p["qk_conv_w"], p["qk_conv_b"], seq)
    zvo = norm_matmul(x, p["attn_norm"], w_vo, BF16)
    zn = proj_nsa(x, p["attn_norm"], w_nsa, cosf, sinf, seq)

    nchunks = seq // MLSTM_CHUNK
    gate_rows = lambda c0: (zg[:, c0:c0 + MLSTM_HEADS].reshape(batch, seq, MLSTM_HEADS)
                            .transpose(0, 2, 1).reshape(batch, MLSTM_HEADS, nchunks, MLSTM_CHUNK))
    lanes = lambda v: jnp.broadcast_to(v.reshape(MLSTM_HEADS, 1, 1), (MLSTM_HEADS, 1, MLSTM_CHUNK))
    mix_a = mlstm_group(zqk, zvo, gate_rows(ZG_MI), gate_rows(ZG_MF), lanes(p["i_bias"]),
                        lanes(p["f_bias"]), p["mlstm_norm"], batch, seq)

    kcmp = compress(_to_groups(zn[:, ZN_KC:ZN_KC + NSA_KV_W], batch, seq), p["cmp_pe_k"],
                    p["cmp_w1_k"].astype(BF16), p["cmp_w2_k"].astype(BF16))
    vcmp = compress(_to_groups(zn[:, ZN_VC:ZN_VC + NSA_KV_W], batch, seq), p["cmp_pe_v"],
                    p["cmp_w1_v"].astype(BF16), p["cmp_w2_v"].astype(BF16))
    o_cmp, sel = cmp_select(zn, kcmp, vcmp, overlap, batch, seq)
    o_slc = sel_attn(zn, onehot, sel, batch, seq)
    o_win = win_attn(zn, batch, seq)
    mix_b = nsa_combine(o_cmp, o_slc, o_win, zg, p["nsa_norm"])

    w_out = p["w_out"].astype(BF16)
    x = matmul_res([mix_a, mix_b], [w_out[:MLSTM_W], w_out[MLSTM_W:]], x, MM_TN)

    act = ffn_up(x, p["ffn_norm"], p["w_up"].astype(BF16), p["ffn_conv_w"], p["ffn_conv_b"], seq)
    x = matmul_res([act], [p["w_down"].astype(BF16)], x, DOWN_TN)
    return x


_LAYER_PARAMS = ("attn_norm", "w_in", "qk_conv_w", "qk_conv_b", "i_bias", "f_bias", "mlstm_norm",
                 "cmp_pe_k", "cmp_pe_v", "cmp_w1_k", "cmp_w2_k", "cmp_w1_v", "cmp_w2_v", "nsa_norm",
                 "w_out", "ffn_norm", "w_up", "ffn_conv_w", "ffn_conv_b", "w_down")


def kernel(x, attn_norm, w_in, qk_conv_w, qk_conv_b, i_bias, f_bias, mlstm_norm, cmp_pe_k, cmp_pe_v,
           cmp_w1_k, cmp_w2_k, cmp_w1_v, cmp_w2_v, nsa_norm, w_out, ffn_norm, w_up, ffn_conv_w,
           ffn_conv_b, w_down, final_norm):
    stacked = dict(zip(_LAYER_PARAMS, (attn_norm, w_in, qk_conv_w, qk_conv_b, i_bias, f_bias,
                                       mlstm_norm, cmp_pe_k, cmp_pe_v, cmp_w1_k, cmp_w2_k, cmp_w1_v,
                                       cmp_w2_v, nsa_norm, w_out, ffn_norm, w_up, ffn_conv_w,
                                       ffn_conv_b, w_down)))
    batch, seq, d = x.shape
    assert d == D_MODEL and seq % MM_TM == 0 and seq % ATTN_TQ == 0 and seq >= WIN_KEYS
    assert seq // CMP_STRIDE == LANES and seq // SEL_BLOCK <= LANES
    cosf, sinf = _rope_tables(seq)
    overlap = _overlap_matrix(seq, seq // CMP_STRIDE)
    onehot = _block_onehot(seq)
    xf = x.reshape(batch * seq, d)
    for layer in range(DEPTH):
        p = {k: v[layer] for k, v in stacked.items()}
        xf = _layer(xf, p, cosf, sinf, overlap, onehot, batch, seq)
    return rmsnorm(xf, final_norm).reshape(batch, seq, d)
```

```python
import functools

import jax
import jax.numpy as jnp
from jax import lax
from jax.experimental import pallas as pl
from jax.experimental.pallas import tpu as pltpu

D_MODEL = 2048
DEPTH = 2
MLSTM_HEADS = 8
HEAD_DIM = 128
MLSTM_W = MLSTM_HEADS * HEAD_DIM
QK_CONV = 4
NSA_HEADS = 8
NSA_KV_HEADS = 2
NSA_GROUP = NSA_HEADS // NSA_KV_HEADS
NSA_W = NSA_HEADS * HEAD_DIM
NSA_KV_W = NSA_KV_HEADS * HEAD_DIM
CMP_BLOCK = 32
CMP_STRIDE = 16
CMP_HIDDEN = 256
SEL_BLOCK = 64
SEL_TOPK = 16
WINDOW = 512
ROPE_THETA = 500000.0
ROPE_DIM = HEAD_DIM // 4
D_FF = 5632
FFN_CONV = 3
NORM_EPS = 1e-6
NEG = -1e30
FORCE_SCORE = 1e4
ATTN_SCALE = HEAD_DIM ** -0.5

LANES = 128
SUBLANES_F32 = 8
SUBLANES_BF16 = 16
VMEM_LIMIT_BYTES = 48 * 1024 * 1024

MM_TM = 1024
MM_TN = 512
EPI_ROWS = 256
DOWN_TN = 256
OUT_TM = 512
MLSTM_CHUNK = 256
MLSTM_HEADS_PER_STEP = 4
ATTN_TQ = 256
ATTN_TK = 256
WIN_KEYS = WINDOW + ATTN_TQ
ROW_TM = 512
CONV_HALO = SUBLANES_BF16

BF16 = jnp.bfloat16
F32 = jnp.float32

W_IN_MI = 4 * MLSTM_W
W_IN_NSA = (W_IN_MI + 2 * MLSTM_HEADS, W_IN_MI + 2 * MLSTM_HEADS + NSA_W + 6 * NSA_KV_W)
W_IN_NG = W_IN_NSA[1]
Z_MQ, Z_MK, Z_MV, Z_MO, Z_NQ = (i * MLSTM_W for i in range(5))
Z_KC, Z_VC, Z_KS, Z_VS, Z_KW, Z_VW = (Z_NQ + NSA_W + i * NSA_KV_W for i in range(6))
Z_W = Z_VW + NSA_KV_W
ZG_MI, ZG_MF, ZG_NG = 0, MLSTM_HEADS, 2 * MLSTM_HEADS


def _cparams(*sem):
    return pltpu.CompilerParams(dimension_semantics=sem, vmem_limit_bytes=VMEM_LIMIT_BYTES)


def _rms(x, gain):
    return x * lax.rsqrt(jnp.mean(x * x, axis=-1, keepdims=True) + NORM_EPS) * gain


def _nt_dot(a, b):
    return lax.dot_general(a, b, (((1,), (1,)), ((), ())), preferred_element_type=F32)


def _stack_heads(q_ref):
    return jnp.concatenate([q_ref[:, g * HEAD_DIM:(g + 1) * HEAD_DIM] for g in range(NSA_GROUP)], axis=0)


def _fill_halo_h(tiles_per_seq, x_ref, xp_ref, g_ref, h_s):
    tm = x_ref.shape[0]
    first = pl.program_id(0) % tiles_per_seq == 0
    hp = _rms(xp_ref[...], g_ref[...])
    h_s[0:CONV_HALO, :] = jnp.where(first, 0.0, hp).astype(h_s.dtype)
    h_s[CONV_HALO:CONV_HALO + tm, :] = _rms(x_ref[...], g_ref[...]).astype(h_s.dtype)


def _causal_conv(u_s, cw_ref, cb_ref, taps, tm):
    acc = jnp.broadcast_to(cb_ref[...], (tm, u_s.shape[1]))
    for j in range(taps):
        off = CONV_HALO - (taps - 1) + j
        acc = acc + u_s[off:off + tm, :] * cw_ref[j:j + 1, :]
    return acc


def _rope(x, cosf, sinf, lane):
    rot = jnp.where(lane < ROPE_DIM // 2, pltpu.roll(x, LANES - ROPE_DIM // 2, axis=1),
                    pltpu.roll(x, ROPE_DIM // 2, axis=1))
    return x * cosf + rot * sinf


def _proj_in_kernel(tiles_per_seq, x_ref, xp_ref, g_ref, w_ref, wg_ref, cw_ref, cb_ref, cos_ref, sin_ref,
                    o_ref, zg_ref, grp_ref, h_s, u_s, y_s):
    tm, tn = o_ref.shape
    j = pl.program_id(1)
    j_k, j_v, j_nq, j_nkv = Z_MK // tn, Z_MV // tn, Z_NQ // tn, Z_KC // tn

    @pl.when(j == 0)
    def _():
        _fill_halo_h(tiles_per_seq, x_ref, xp_ref, g_ref, h_s)
        zg_ref[...] = jnp.dot(h_s[CONV_HALO:CONV_HALO + tm, :], wg_ref[...], preferred_element_type=F32)
        grp_ref[...] = jnp.zeros_like(grp_ref)

    @pl.when(j < j_v)
    def _():
        u_s[...] = jnp.dot(h_s[...], w_ref[...], preferred_element_type=F32)
        y = _causal_conv(u_s, cw_ref, cb_ref, QK_CONV, tm)
        scale = jnp.where(j >= j_k, ATTN_SCALE, 1.0)
        o_ref[...] = (y * jax.nn.sigmoid(y) * scale).astype(o_ref.dtype)

    @pl.when((j >= j_v) & (j < j_nq))
    def _():
        o_ref[...] = jnp.dot(h_s[CONV_HALO:CONV_HALO + tm, :], w_ref[...],
                             preferred_element_type=F32).astype(o_ref.dtype)

    @pl.when(j >= j_nq)
    def _():
        is_q = j < j_nkv
        for r in range(tm // EPI_ROWS):
            rs = slice(r * EPI_ROWS, (r + 1) * EPI_ROWS)
            cosf, sinf = cos_ref[rs, :], sin_ref[rs, :]
            lane = lax.broadcasted_iota(jnp.int32, cosf.shape, 1)
            u = jnp.dot(h_s[CONV_HALO + r * EPI_ROWS:CONV_HALO + (r + 1) * EPI_ROWS, :], w_ref[...],
                        preferred_element_type=F32)
            for head in range(tn // HEAD_DIM):
                cs = slice(head * HEAD_DIM, (head + 1) * HEAD_DIM)
                xh = u[:, cs]
                roped = _rope(xh, cosf, sinf, lane)
                if head < NSA_KV_HEADS:
                    yh = roped * jnp.where(is_q, ATTN_SCALE, 1.0)
                else:
                    yh = jnp.where(is_q, roped * ATTN_SCALE, xh)
                o_ref[rs, cs] = yh.astype(o_ref.dtype)
                y_s[head, rs, :] = yh

        @pl.when(j == j_nkv)
        def _():
            ngrp = tm // CMP_STRIDE
            for head in range(tn // HEAD_DIM):
                for l in range(CMP_STRIDE):
                    col = (head * CMP_STRIDE + l) * HEAD_DIM
                    grp_ref[:, col:col + HEAD_DIM] = y_s[head, pl.ds(l, ngrp, stride=CMP_STRIDE),
                                                         :].astype(grp_ref.dtype)


def proj_in(x, g, w_z, w_gate, conv_w, conv_b, cosf, sinf, seq):
    m, k = x.shape
    tm, tn, halo = MM_TM, MM_TN, CONV_HALO
    assert tn == 2 * NSA_KV_W and Z_MV % tn == 0 and Z_NQ % tn == 0 and Z_KC % tn == 0
    n_conv = Z_MV // tn
    nper = seq // tm
    conv_tile = lambda j: jnp.minimum(j, n_conv - 1)
    return pl.pallas_call(
        functools.partial(_proj_in_kernel, seq // tm),
        out_shape=(jax.ShapeDtypeStruct((m, Z_W), BF16), jax.ShapeDtypeStruct((m, LANES), F32),
                   jax.ShapeDtypeStruct((m // CMP_STRIDE, tn * CMP_STRIDE), BF16)),
        grid=(m // tm, Z_W // tn),
        in_specs=[
            pl.BlockSpec((tm, k), lambda i, j: (i, 0)),
            pl.BlockSpec((halo, k), lambda i, j: (jnp.maximum(i * (tm // halo) - 1, 0), 0)),
            pl.BlockSpec((1, k), lambda i, j: (0, 0)),
            pl.BlockSpec((k, tn), lambda i, j: (0, j)),
            pl.BlockSpec((k, LANES), lambda i, j: (0, 0)),
            pl.BlockSpec((QK_CONV, tn), lambda i, j: (0, conv_tile(j))),
            pl.BlockSpec((1, tn), lambda i, j: (0, conv_tile(j))),
            pl.BlockSpec((tm, LANES), lambda i, j: (i % nper, 0)),
            pl.BlockSpec((tm, LANES), lambda i, j: (i % nper, 0)),
        ],
        out_specs=(pl.BlockSpec((tm, tn), lambda i, j: (i, j)),
                   pl.BlockSpec((tm, LANES), lambda i, j: (i, 0)),
                   pl.BlockSpec((tm // CMP_STRIDE, tn * CMP_STRIDE), lambda i, j: (i, 0))),
        scratch_shapes=[pltpu.VMEM((tm + halo, k), BF16), pltpu.VMEM((tm + halo, tn), F32),
                        pltpu.VMEM((tn // HEAD_DIM, tm, HEAD_DIM), F32)],
        compiler_params=_cparams("parallel", "arbitrary"),
        name="proj_in",
    )(x, x, g.reshape(1, k), w_z, w_gate, conv_w, conv_b.reshape(1, -1), cosf, sinf)


def _matmul_res_kernel(n_lhs, *refs):
    lhs = refs[:n_lhs]
    ws = refs[n_lhs:2 * n_lhs]
    res_ref, o_ref = refs[2 * n_lhs], refs[2 * n_lhs + 1]
    acc = res_ref[...]
    for a_ref, w_ref in zip(lhs, ws):
        acc = acc + jnp.dot(a_ref[...], w_ref[...], preferred_element_type=F32)
    o_ref[...] = acc


def matmul_res(lhs_list, w_list, res, tm, tn):
    m, n = res.shape
    n_lhs = len(lhs_list)
    in_specs = [pl.BlockSpec((tm, a.shape[1]), lambda i, j: (i, 0)) for a in lhs_list]
    in_specs += [pl.BlockSpec((w.shape[0], tn), lambda i, j: (0, j)) for w in w_list]
    in_specs += [pl.BlockSpec((tm, tn), lambda i, j: (i, j))]
    return pl.pallas_call(
        functools.partial(_matmul_res_kernel, n_lhs),
        out_shape=jax.ShapeDtypeStruct((m, n), F32),
        grid=(m // tm, n // tn),
        in_specs=in_specs,
        out_specs=pl.BlockSpec((tm, tn), lambda i, j: (i, j)),
        compiler_params=_cparams("parallel", "arbitrary"),
        name="matmul_res",
    )(*lhs_list, *w_list, res)


def _mlstm_kernel(seq, q_ref, k_ref, v_ref, zo_ref, gi_ref, gf_ref, bi_ref, bf_ref, nrm_ref, o_ref,
                  b_s, i_s, cx_s, m_s):
    L = MLSTM_CHUNK
    hps = MLSTM_HEADS_PER_STEP
    nchunks = seq // L

    lane = lax.broadcasted_iota(jnp.int32, (nchunks, L), 1)
    for hh in range(hps):
        i_s[hh] = gi_ref[0, hh] + bi_ref[hh]
        fpre = gf_ref[0, hh] + bf_ref[hh]
        csum = jnp.minimum(fpre, 0.0) - jnp.log1p(jnp.exp(-jnp.abs(fpre)))
        for sh in (1 << e for e in range(L.bit_length() - 1)):
            csum = csum + jnp.where(lane >= sh, pltpu.roll(csum, sh, axis=1), 0.0)
        b_s[hh] = csum

    cx_s[...] = jnp.zeros_like(cx_s)
    m_s[...] = jnp.zeros_like(m_s)

    row = lax.broadcasted_iota(jnp.int32, (L, L), 0)
    col = lax.broadcasted_iota(jnp.int32, (L, L), 1)
    eye = row == col
    causal = col <= row
    ones = jnp.ones((L, LANES), BF16)

    def chunk_body(c, carry):
        r0 = pl.multiple_of(c * L, L)
        for hh in range(hps):
            cs = slice(hh * HEAD_DIM, (hh + 1) * HEAD_DIM)
            qc = q_ref[pl.ds(r0, L), cs]
            kc = k_ref[pl.ds(r0, L), cs]
            vx = jnp.concatenate([v_ref[pl.ds(r0, L), cs], ones], axis=1)
            b_row = b_s[hh, pl.ds(c, 1), :]
            u_row = i_s[hh, pl.ds(c, 1), :] - b_row
            b_col = jnp.sum(jnp.where(eye, b_row, 0.0), axis=-1, keepdims=True)
            m_prev = m_s[hh]
            cx = cx_s[hh]

            umat = jnp.where(causal, u_row, -jnp.inf)
            m_col = jnp.maximum(m_prev, jnp.max(umat, axis=-1, keepdims=True))
            w_inter = jnp.exp(m_prev - m_col)
            s_qk = _nt_dot(qc, kc) * jnp.exp(umat - m_col)
            nd = (w_inter * jnp.dot(qc, cx.astype(BF16), preferred_element_type=F32)
                  + jnp.dot(s_qk.astype(BF16), vx, preferred_element_type=F32))
            floor = jnp.exp(-(b_col + m_col))
            h = nd[:, :HEAD_DIM] / jnp.maximum(jnp.abs(nd[:, HEAD_DIM:]), floor)

            m_last = m_col[L - 1:L, :]
            kw_t = kc.astype(F32).T * jnp.exp(u_row - m_last)
            cx_s[hh] = (jnp.exp(m_prev - m_last) * cx
                        + jnp.dot(kw_t.astype(BF16), vx, preferred_element_type=F32))
            m_s[hh] = b_row[:, L - 1:L] + m_last

            hn = _rms(h, nrm_ref[:, cs])
            gate = jax.nn.sigmoid(zo_ref[pl.ds(r0, L), cs].astype(F32))
            o_ref[pl.ds(r0, L), cs] = (hn * gate).astype(o_ref.dtype)
        return carry

    lax.fori_loop(0, nchunks, chunk_body, 0)


def mlstm_group(z, gi, gf, bi, bf, norm_g, batch, seq):
    hps = MLSTM_HEADS_PER_STEP
    wblk = hps * HEAD_DIM
    nchunks = seq // MLSTM_CHUNK
    nhb = MLSTM_HEADS // hps
    zspec = lambda col0: pl.BlockSpec((seq, wblk), lambda b, h: (b, col0 // wblk + h))
    gspec = pl.BlockSpec((1, hps, nchunks, MLSTM_CHUNK), lambda b, h: (b, h, 0, 0))
    bspec = pl.BlockSpec((hps, 1, MLSTM_CHUNK), lambda b, h: (h, 0, 0))
    return pl.pallas_call(
        functools.partial(_mlstm_kernel, seq),
        out_shape=jax.ShapeDtypeStruct((batch * seq, MLSTM_W), BF16),
        grid=(batch, nhb),
        in_specs=[zspec(Z_MQ), zspec(Z_MK), zspec(Z_MV), zspec(Z_MO), gspec, gspec, bspec, bspec,
                  pl.BlockSpec((1, wblk), lambda b, h: (0, h))],
        out_specs=pl.BlockSpec((seq, wblk), lambda b, h: (b, h)),
        scratch_shapes=[
            pltpu.VMEM((hps, nchunks, MLSTM_CHUNK), F32),
            pltpu.VMEM((hps, nchunks, MLSTM_CHUNK), F32),
            pltpu.VMEM((hps, HEAD_DIM, 2 * HEAD_DIM), F32),
            pltpu.VMEM((hps, 1, 1), F32),
        ],
        compiler_params=_cparams("parallel", "parallel"),
        name="mlstm_group",
    )(z, z, z, z, gi, gf, bi, bf, norm_g.reshape(1, -1))


def _compress_kernel(x_ref, pe_ref, w1_ref, w2_ref, o_ref, sh_s):
    half = CMP_STRIDE * HEAD_DIM
    ngroups = x_ref.shape[0]
    x = x_ref[...].astype(F32)
    lo = jnp.dot((x + pe_ref[:, :half]).astype(BF16), w1_ref[:half, :], preferred_element_type=F32)
    hi = jnp.dot((x + pe_ref[:, half:]).astype(BF16), w1_ref[half:, :], preferred_element_type=F32)
    sh_s[0:ngroups, :] = hi
    sh_s[ngroups:ngroups + SUBLANES_F32, :] = jnp.zeros((SUBLANES_F32, CMP_HIDDEN), F32)
    pre = lo + sh_s[1:ngroups + 1, :]
    act = jax.nn.gelu(pre, approximate=True)
    o_ref[0] = jnp.dot(act.astype(BF16), w2_ref[...], preferred_element_type=F32).astype(o_ref.dtype)


def compress(grp, first_head, pe, w1, w2, batch, seq):
    nslab, ngroups, gw = batch * NSA_KV_HEADS, seq // CMP_STRIDE, CMP_STRIDE * HEAD_DIM
    return pl.pallas_call(
        _compress_kernel,
        out_shape=jax.ShapeDtypeStruct((nslab, ngroups, HEAD_DIM), BF16),
        grid=(nslab,),
        in_specs=[pl.BlockSpec((ngroups, gw), lambda s: (s // NSA_KV_HEADS, first_head + s % NSA_KV_HEADS)),
                  pl.BlockSpec((1, 2 * gw), lambda s: (0, 0)),
                  pl.BlockSpec((2 * gw, CMP_HIDDEN), lambda s: (0, 0)),
                  pl.BlockSpec((CMP_HIDDEN, HEAD_DIM), lambda s: (0, 0))],
        out_specs=pl.BlockSpec((1, ngroups, HEAD_DIM), lambda s: (s, 0, 0)),
        scratch_shapes=[pltpu.VMEM((ngroups + SUBLANES_F32, CMP_HIDDEN), F32)],
        compiler_params=_cparams("parallel"),
        name="nsa_compress",
    )(grp, pe.reshape(1, -1), w1, w2)


def _cmp_select_kernel(nsel, q_ref, kc_ref, vc_ref, ov_ref, o_ref, sel_ref):
    tq = ATTN_TQ
    s0 = pl.program_id(2) * tq
    kc = kc_ref[0]
    vc = vc_ref[0]
    ncp = kc.shape[0]
    pos = s0 + lax.broadcasted_iota(jnp.int32, (tq, ncp), 0)
    cidx = lax.broadcasted_iota(jnp.int32, (tq, ncp), 1)
    valid = cidx * CMP_STRIDE + (CMP_BLOCK - 1) <= pos
    any_valid = s0 + lax.broadcasted_iota(jnp.int32, (tq, 1), 0) >= CMP_BLOCK - 1
    sc = _nt_dot(_stack_heads(q_ref), kc)
    sc = jnp.where(valid[None], sc.reshape(NSA_GROUP, tq, ncp), NEG)
    e = jnp.where(valid[None], jnp.exp(sc - jnp.max(sc, axis=-1, keepdims=True)), 0.0)
    denom = jnp.sum(e, axis=-1, keepdims=True)
    p = e / jnp.where(any_valid[None], denom, 1.0)
    o = jnp.dot(p.reshape(NSA_GROUP * tq, ncp).astype(BF16), vc, preferred_element_type=F32)
    p_sum = p[0]
    for g in range(NSA_GROUP):
        o_ref[:, g * HEAD_DIM:(g + 1) * HEAD_DIM] = o[g * tq:(g + 1) * tq, :].astype(o_ref.dtype)
        if g:
            p_sum = p_sum + p[g]
    imp = jnp.dot(p_sum, ov_ref[...], preferred_element_type=F32, precision=lax.Precision.HIGHEST)
    jb = cidx
    cur = pos // SEL_BLOCK
    forced = (jb == 0) | (jb == cur) | (jb == cur - 1)
    imp = jnp.where(forced, FORCE_SCORE, imp)
    imp = jnp.where(jb <= cur, imp, -1.0)
    imp_t = imp.T[:nsel, :]
    jb_t = lax.broadcasted_iota(jnp.int32, (nsel, tq), 0)
    rank = jnp.zeros((nsel, tq), F32)
    for j2 in range(nsel):
        rowv = imp_t[j2:j2 + 1, :]
        before = (rowv > imp_t) | ((rowv == imp_t) & (jb_t > j2))
        rank = rank + jnp.where(before, 1.0, 0.0)
    sel_t = jnp.where(rank < float(SEL_TOPK), 1.0, 0.0)
    sel_t = jnp.concatenate([sel_t, jnp.zeros((ncp - nsel, tq), F32)], axis=0)
    sel_ref[...] = sel_t.T.astype(sel_ref.dtype)


def cmp_select(z, kcmp, vcmp, overlap, batch, seq):
    m = z.shape[0]
    tq = ATTN_TQ
    nq = seq // tq
    ncp = kcmp.shape[1]
    gw = NSA_GROUP * HEAD_DIM
    return pl.pallas_call(
        functools.partial(_cmp_select_kernel, seq // SEL_BLOCK),
        out_shape=(jax.ShapeDtypeStruct((m, NSA_W), BF16),
                   jax.ShapeDtypeStruct((m, NSA_KV_HEADS * LANES), BF16)),
        grid=(batch, NSA_KV_HEADS, nq),
        in_specs=[pl.BlockSpec((tq, gw), lambda b, h, i: (b * nq + i, Z_NQ // gw + h)),
                  pl.BlockSpec((1, ncp, HEAD_DIM), lambda b, h, i: (b * NSA_KV_HEADS + h, 0, 0)),
                  pl.BlockSpec((1, ncp, HEAD_DIM), lambda b, h, i: (b * NSA_KV_HEADS + h, 0, 0)),
                  pl.BlockSpec((ncp, LANES), lambda b, h, i: (0, 0))],
        out_specs=(pl.BlockSpec((tq, gw), lambda b, h, i: (b * nq + i, h)),
                   pl.BlockSpec((tq, LANES), lambda b, h, i: (b * nq + i, h))),
        compiler_params=_cparams("parallel", "parallel", "parallel"),
        name="nsa_cmp_select",
    )(z, kcmp, vcmp, overlap)


def _sel_attn_kernel(q_ref, k_ref, v_ref, oh_ref, sel_ref, o_ref, qx_s, m_s, acc_s, s0_s, s1_s):
    tq, tk = ATTN_TQ, ATTN_TK
    i = pl.program_id(2)
    penalty = ((sel_ref[...].astype(F32) - 1.0) * (-NEG)).astype(BF16)
    for g in range(NSA_GROUP):
        cs = slice(g * HEAD_DIM, (g + 1) * HEAD_DIM)
        qx_s[g * tq:(g + 1) * tq, :] = jnp.concatenate([q_ref[:, cs], penalty], axis=1)
    m_s[...] = jnp.full_like(m_s, NEG)
    acc_s[...] = jnp.zeros_like(acc_s)
    ones = jnp.ones((tk, LANES), BF16)

    def scores(t, s_buf):
        k0 = pl.multiple_of(t * tk, tk)
        kx = jnp.concatenate([k_ref[pl.ds(k0, tk), :], oh_ref[pl.ds(k0, tk), :]], axis=1)
        s_buf[...] = _nt_dot(qx_s[...], kx)

    def update(t, s_buf, diagonal):
        k0 = pl.multiple_of(t * tk, tk)
        vx = jnp.concatenate([v_ref[pl.ds(k0, tk), :], ones], axis=1)
        s = s_buf[...]
        if diagonal:
            row = lax.broadcasted_iota(jnp.int32, (tq, tk), 0)
            col = lax.broadcasted_iota(jnp.int32, (tq, tk), 1)
            causal = jnp.where(col <= row, 0.0, NEG)
            s = (s.reshape(NSA_GROUP, tq, tk) + causal[None]).reshape(NSA_GROUP * tq, tk)
        m_old = m_s[...]
        m_new = jnp.maximum(m_old, jnp.max(s, axis=-1, keepdims=True))
        alpha = jnp.exp(m_old - m_new)
        p = jnp.exp((s - jnp.concatenate([m_new] * (tk // LANES), axis=1)).astype(BF16))
        acc_s[...] = (jnp.concatenate([alpha, alpha], axis=1) * acc_s[...]
                      + jnp.dot(p, vx, preferred_element_type=F32))
        m_s[...] = m_new

    scores(0, s0_s)

    def pair(p, carry):
        t = 2 * p
        scores(t + 1, s1_s)
        update(t, s0_s, False)
        scores(t + 2, s0_s)
        update(t + 1, s1_s, False)
        return carry

    lax.fori_loop(0, i // 2, pair, 0)

    @pl.when(i % 2 == 0)
    def _():
        update(i, s0_s, True)

    @pl.when(i % 2 == 1)
    def _():
        scores(i, s1_s)
        update(i - 1, s0_s, False)
        update(i, s1_s, True)
    for g in range(NSA_GROUP):
        cs = slice(g * HEAD_DIM, (g + 1) * HEAD_DIM)
        acc = acc_s[g * tq:(g + 1) * tq, :]
        o_ref[:, cs] = (acc[:, :HEAD_DIM] / acc[:, HEAD_DIM:]).astype(o_ref.dtype)


def _kv_spec(seq, col0):
    return pl.BlockSpec((seq, HEAD_DIM), lambda b, h, i: (b, col0 // HEAD_DIM + h))


def sel_attn(z, onehot, sel, batch, seq):
    assert ATTN_TQ == ATTN_TK
    m = z.shape[0]
    tq = ATTN_TQ
    nq = seq // tq
    gw = NSA_GROUP * HEAD_DIM
    return pl.pallas_call(
        _sel_attn_kernel,
        out_shape=jax.ShapeDtypeStruct((m, NSA_W), BF16),
        grid=(batch, NSA_KV_HEADS, nq),
        in_specs=[pl.BlockSpec((tq, gw), lambda b, h, i: (b * nq + i, Z_NQ // gw + h)),
                  _kv_spec(seq, Z_KS), _kv_spec(seq, Z_VS),
                  pl.BlockSpec((seq, LANES), lambda b, h, i: (0, 0)),
                  pl.BlockSpec((tq, LANES), lambda b, h, i: (b * nq + i, h))],
        out_specs=pl.BlockSpec((tq, gw), lambda b, h, i: (b * nq + i, h)),
        scratch_shapes=[pltpu.VMEM((NSA_GROUP * tq, 2 * LANES), BF16),
                        pltpu.VMEM((NSA_GROUP * tq, LANES), F32),
                        pltpu.VMEM((NSA_GROUP * tq, 2 * LANES), F32),
                        pltpu.VMEM((NSA_GROUP * tq, ATTN_TK), F32),
                        pltpu.VMEM((NSA_GROUP * tq, ATTN_TK), F32)],
        compiler_params=_cparams("parallel", "parallel", "arbitrary"),
        name="nsa_sel_attn",
    )(z, z, z, onehot, sel)


def _win_attn_kernel(q_ref, k_ref, v_ref, o_ref):
    tq, nk = ATTN_TQ, WIN_KEYS
    s0 = pl.program_id(2) * tq
    start = pl.multiple_of(jnp.maximum(s0 - WINDOW, 0), tq)
    kw = k_ref[pl.ds(start, nk), :]
    vx = jnp.concatenate([v_ref[pl.ds(start, nk), :], jnp.ones((nk, LANES), BF16)], axis=1)
    qpos = s0 + lax.broadcasted_iota(jnp.int32, (tq, nk), 0)
    kpos = start + lax.broadcasted_iota(jnp.int32, (tq, nk), 1)
    diff = qpos - kpos
    bias = jnp.where((diff >= 0) & (diff < WINDOW), 0.0, NEG)
    for g in range(NSA_GROUP):
        cs = slice(g * HEAD_DIM, (g + 1) * HEAD_DIM)
        s = _nt_dot(q_ref[:, cs], kw) + bias
        p = jnp.exp((s - jnp.max(s, axis=-1, keepdims=True)).astype(BF16))
        a = jnp.dot(p, vx, preferred_element_type=F32)
        o_ref[:, cs] = (a[:, :HEAD_DIM] / a[:, HEAD_DIM:]).astype(o_ref.dtype)


def win_attn(z, batch, seq):
    m = z.shape[0]
    tq = ATTN_TQ
    nq = seq // tq
    gw = NSA_GROUP * HEAD_DIM
    return pl.pallas_call(
        _win_attn_kernel,
        out_shape=jax.ShapeDtypeStruct((m, NSA_W), BF16),
        grid=(batch, NSA_KV_HEADS, nq),
        in_specs=[pl.BlockSpec((tq, gw), lambda b, h, i: (b * nq + i, Z_NQ // gw + h)),
                  _kv_spec(seq, Z_KW), _kv_spec(seq, Z_VW)],
        out_specs=pl.BlockSpec((tq, gw), lambda b, h, i: (b * nq + i, h)),
        compiler_params=_cparams("parallel", "parallel", "parallel"),
        name="nsa_win_attn",
    )(z, z, z)


def _nsa_combine_kernel(oc_ref, os_ref, ow_ref, zg_ref, nrm_ref, o_ref, mix_s):
    tm = oc_ref.shape[0]
    gates = jax.nn.sigmoid(zg_ref[...])
    ssq = jnp.zeros((tm, 1), F32)
    for h in range(NSA_HEADS):
        cs = slice(h * HEAD_DIM, (h + 1) * HEAD_DIM)
        c0 = ZG_NG + 3 * h
        mix = (gates[:, c0:c0 + 1] * oc_ref[:, cs].astype(F32)
               + gates[:, c0 + 1:c0 + 2] * os_ref[:, cs].astype(F32)
               + gates[:, c0 + 2:c0 + 3] * ow_ref[:, cs].astype(F32))
        mix_s[:, cs] = mix
        ssq = ssq + jnp.sum(mix * mix, axis=-1, keepdims=True)
    inv = lax.rsqrt(ssq / NSA_W + NORM_EPS)
    o_ref[...] = (mix_s[...] * inv * nrm_ref[...]).astype(o_ref.dtype)


def nsa_combine(o_cmp, o_slc, o_win, zg, norm_g):
    m = o_cmp.shape[0]
    tm = ROW_TM
    ospec = pl.BlockSpec((tm, NSA_W), lambda i: (i, 0))
    return pl.pallas_call(
        _nsa_combine_kernel,
        out_shape=jax.ShapeDtypeStruct((m, NSA_W), BF16),
        grid=(m // tm,),
        in_specs=[ospec, ospec, ospec, pl.BlockSpec((tm, LANES), lambda i: (i, 0)),
                  pl.BlockSpec((1, NSA_W), lambda i: (0, 0))],
        out_specs=ospec,
        scratch_shapes=[pltpu.VMEM((tm, NSA_W), F32)],
        compiler_params=_cparams("parallel"),
        name="nsa_combine",
    )(o_cmp, o_slc, o_win, zg, norm_g.reshape(1, -1))


def _ffn_up_kernel(tiles_per_seq, x_ref, xp_ref, g_ref, wg_ref, wu_ref, cwg_ref, cwu_ref,
                   cbg_ref, cbu_ref, o_ref, h_s, ug_s, uu_s):
    tm = x_ref.shape[0]

    @pl.when(pl.program_id(1) == 0)
    def _():
        _fill_halo_h(tiles_per_seq, x_ref, xp_ref, g_ref, h_s)

    ug_s[...] = jnp.dot(h_s[...], wg_ref[...], preferred_element_type=F32)
    uu_s[...] = jnp.dot(h_s[...], wu_ref[...], preferred_element_type=F32)
    gate = _causal_conv(ug_s, cwg_ref, cbg_ref, FFN_CONV, tm)
    up = _causal_conv(uu_s, cwu_ref, cbu_ref, FFN_CONV, tm)
    o_ref[...] = (gate * jax.nn.sigmoid(gate) * up).astype(o_ref.dtype)


def ffn_up(x, g, w_up, conv_w, conv_b, seq):
    m, k = x.shape
    tm, tn, halo = MM_TM, MM_TN, CONV_HALO
    ntile = D_FF // tn
    cb = conv_b.reshape(1, -1)
    return pl.pallas_call(
        functools.partial(_ffn_up_kernel, seq // tm),
        out_shape=jax.ShapeDtypeStruct((m, D_FF), BF16),
        grid=(m // tm, ntile),
        in_specs=[
            pl.BlockSpec((tm, k), lambda i, j: (i, 0)),
            pl.BlockSpec((halo, k), lambda i, j: (jnp.maximum(i * (tm // halo) - 1, 0), 0)),
            pl.BlockSpec((1, k), lambda i, j: (0, 0)),
            pl.BlockSpec((k, tn), lambda i, j: (0, j)),
            pl.BlockSpec((k, tn), lambda i, j: (0, ntile + j)),
            pl.BlockSpec((FFN_CONV, tn), lambda i, j: (0, j)),
            pl.BlockSpec((FFN_CONV, tn), lambda i, j: (0, ntile + j)),
            pl.BlockSpec((1, tn), lambda i, j: (0, j)),
            pl.BlockSpec((1, tn), lambda i, j: (0, ntile + j)),
        ],
        out_specs=pl.BlockSpec((tm, tn), lambda i, j: (i, j)),
        scratch_shapes=[pltpu.VMEM((tm + halo, k), BF16),
                        pltpu.VMEM((tm + halo, tn), F32),
                        pltpu.VMEM((tm + halo, tn), F32)],
        compiler_params=_cparams("parallel", "arbitrary"),
        name="ffn_up",
    )(x, x, g.reshape(1, k), w_up, w_up, conv_w, conv_w, cb, cb)


def _rmsnorm_kernel(x_ref, g_ref, o_ref):
    o_ref[...] = _rms(x_ref[...], g_ref[...])


def rmsnorm(x, g):
    m, k = x.shape
    tm = ROW_TM
    return pl.pallas_call(
        _rmsnorm_kernel,
        out_shape=jax.ShapeDtypeStruct((m, k), F32),
        grid=(m // tm,),
        in_specs=[pl.BlockSpec((tm, k), lambda i: (i, 0)), pl.BlockSpec((1, k), lambda i: (0, 0))],
        out_specs=pl.BlockSpec((tm, k), lambda i: (i, 0)),
        compiler_params=_cparams("parallel"),
        name="final_rmsnorm",
    )(x, g.reshape(1, k))


def _gate_weight(w_in):
    gates = jnp.concatenate([w_in[:, W_IN_MI:W_IN_MI + 2 * MLSTM_HEADS],
                             w_in[:, W_IN_NG:W_IN_NG + 3 * NSA_HEADS]], axis=1)
    return jnp.pad(gates, ((0, 0), (0, LANES - gates.shape[1]))).astype(BF16)


def _rope_tables(seq):
    pos = jnp.arange(seq, dtype=F32)
    inv = ROPE_THETA ** (-jnp.arange(0, ROPE_DIM, 2, dtype=F32) / ROPE_DIM)
    ang = pos[:, None] * inv[None, :]
    cos, sin = jnp.cos(ang), jnp.sin(ang)
    rest = HEAD_DIM - ROPE_DIM
    cosf = jnp.concatenate([cos, cos, jnp.ones((seq, rest), F32)], axis=1)
    sinf = jnp.concatenate([-sin, sin, jnp.zeros((seq, rest), F32)], axis=1)
    return cosf, sinf


def _overlap_matrix(seq, ncmp_pad):
    nsel = seq // SEL_BLOCK
    cstart = jnp.arange(ncmp_pad) * CMP_STRIDE
    jb = jnp.arange(LANES)
    ov = ((cstart[:, None] < (jb[None, :] + 1) * SEL_BLOCK)
          & (cstart[:, None] + CMP_BLOCK > jb[None, :] * SEL_BLOCK) & (jb[None, :] < nsel))
    return ov.astype(F32)


def _block_onehot(seq):
    blk = jnp.arange(seq)[:, None] // SEL_BLOCK
    return (blk == jnp.arange(LANES)[None, :]).astype(BF16)


def _layer(x, p, cosf, sinf, overlap, onehot, batch, seq):
    w_in = p["w_in"]
    w_z = jnp.concatenate([w_in[:, :W_IN_MI], w_in[:, W_IN_NSA[0]:W_IN_NSA[1]]], axis=1).astype(BF16)
    z, zg, grp = proj_in(x, p["attn_norm"], w_z, _gate_weight(w_in), p["qk_conv_w"], p["qk_conv_b"],
                         cosf, sinf, seq)

    nchunks = seq // MLSTM_CHUNK
    gate_rows = lambda c0: (zg[:, c0:c0 + MLSTM_HEADS].reshape(batch, seq, MLSTM_HEADS)
                            .transpose(0, 2, 1).reshape(batch, MLSTM_HEADS, nchunks, MLSTM_CHUNK))
    lanes = lambda v: jnp.broadcast_to(v.reshape(MLSTM_HEADS, 1, 1), (MLSTM_HEADS, 1, MLSTM_CHUNK))
    mix_a = mlstm_group(z, gate_rows(ZG_MI), gate_rows(ZG_MF), lanes(p["i_bias"]),
                        lanes(p["f_bias"]), p["mlstm_norm"], batch, seq)

    kcmp = compress(grp, 0, p["cmp_pe_k"], p["cmp_w1_k"].astype(BF16), p["cmp_w2_k"].astype(BF16),
                    batch, seq)
    vcmp = compress(grp, NSA_KV_HEADS, p["cmp_pe_v"], p["cmp_w1_v"].astype(BF16),
                    p["cmp_w2_v"].astype(BF16), batch, seq)
    o_cmp, sel = cmp_select(z, kcmp, vcmp, overlap, batch, seq)
    o_slc = sel_attn(z, onehot, sel, batch, seq)
    o_win = win_attn(z, batch, seq)
    mix_b = nsa_combine(o_cmp, o_slc, o_win, zg, p["nsa_norm"])

    w_out = p["w_out"].astype(BF16)
    x = matmul_res([mix_a, mix_b], [w_out[:MLSTM_W], w_out[MLSTM_W:]], x, OUT_TM, D_MODEL)

    act = ffn_up(x, p["ffn_norm"], p["w_up"].astype(BF16), p["ffn_conv_w"], p["ffn_conv_b"], seq)
    x = matmul_res([act], [p["w_down"].astype(BF16)], x, MM_TM, DOWN_TN)
    return x


_LAYER_PARAMS = ("attn_norm", "w_in", "qk_conv_w", "qk_conv_b", "i_bias", "f_bias", "mlstm_norm",
                 "cmp_pe_k", "cmp_pe_v", "cmp_w1_k", "cmp_w2_k", "cmp_w1_v", "cmp_w2_v", "nsa_norm",
                 "w_out", "ffn_norm", "w_up", "ffn_conv_w", "ffn_conv_b", "w_down")


def kernel(x, attn_norm, w_in, qk_conv_w, qk_conv_b, i_bias, f_bias, mlstm_norm, cmp_pe_k, cmp_pe_v,
           cmp_w1_k, cmp_w2_k, cmp_w1_v, cmp_w2_v, nsa_norm, w_out, ffn_norm, w_up, ffn_conv_w,
           ffn_conv_b, w_down, final_norm):
    stacked = dict(zip(_LAYER_PARAMS, (attn_norm, w_in, qk_conv_w, qk_conv_b, i_bias, f_bias,
                                       mlstm_norm, cmp_pe_k, cmp_pe_v, cmp_w1_k, cmp_w2_k, cmp_w1_v,
                                       cmp_w2_v, nsa_norm, w_out, ffn_norm, w_up, ffn_conv_w,
                                       ffn_conv_b, w_down)))
    batch, seq, d = x.shape
    assert d == D_MODEL and seq % MM_TM == 0 and seq % ATTN_TQ == 0 and seq >= WIN_KEYS
    assert seq // CMP_STRIDE == LANES and seq // SEL_BLOCK <= LANES
    cosf, sinf = _rope_tables(seq)
    overlap = _overlap_matrix(seq, seq // CMP_STRIDE)
    onehot = _block_onehot(seq)
    xf = x.reshape(batch * seq, d)
    for layer in range(DEPTH):
        p = {k: v[layer] for k, v in stacked.items()}
        xf = _layer(xf, p, cosf, sinf, overlap, onehot, batch, seq)
    return rmsnorm(xf, final_norm).reshape(batch, seq, d)
```

```python
import functools

import jax
import jax.numpy as jnp
from jax import lax
from jax.experimental import pallas as pl
from jax.experimental.pallas import tpu as pltpu

D_MODEL = 2048
DEPTH = 2
MLSTM_HEADS = 8
HEAD_DIM = 128
MLSTM_W = MLSTM_HEADS * HEAD_DIM
QK_CONV = 4
NSA_HEADS = 8
NSA_KV_HEADS = 2
NSA_GROUP = NSA_HEADS // NSA_KV_HEADS
NSA_W = NSA_HEADS * HEAD_DIM
NSA_KV_W = NSA_KV_HEADS * HEAD_DIM
CMP_BLOCK = 32
CMP_STRIDE = 16
CMP_HIDDEN = 256
SEL_BLOCK = 64
SEL_TOPK = 16
WINDOW = 512
ROPE_THETA = 500000.0
ROPE_DIM = HEAD_DIM // 4
D_FF = 5632
FFN_CONV = 3
NORM_EPS = 1e-6
NEG = -1e30
FORCE_SCORE = 1e4
ATTN_SCALE = HEAD_DIM ** -0.5

LANES = 128
SUBLANES_F32 = 8
SUBLANES_BF16 = 16
VMEM_LIMIT_BYTES = 48 * 1024 * 1024

MM_TM = 1024
MM_TN = 512
EPI_ROWS = 256
DOWN_TN = 512
OUT_TM = 512
MLSTM_CHUNK = 256
MLSTM_HEADS_PER_STEP = 4
ATTN_TQ = 256
CMP_TQ = 512
ATTN_TK = 256
WIN_KEYS = WINDOW + ATTN_TQ
ROW_TM = 512
CONV_HALO = SUBLANES_BF16

BF16 = jnp.bfloat16
F32 = jnp.float32

W_IN_MI = 4 * MLSTM_W
W_IN_NSA = (W_IN_MI + 2 * MLSTM_HEADS, W_IN_MI + 2 * MLSTM_HEADS + NSA_W + 6 * NSA_KV_W)
W_IN_NG = W_IN_NSA[1]
Z_MQ, Z_MK, Z_MV, Z_MO, Z_NQ = (i * MLSTM_W for i in range(5))
Z_KC, Z_VC, Z_KS, Z_VS, Z_KW, Z_VW = (Z_NQ + NSA_W + i * NSA_KV_W for i in range(6))
Z_W = Z_VW + NSA_KV_W
ZG_MI, ZG_MF, ZG_NG = 0, MLSTM_HEADS, 2 * MLSTM_HEADS


def _cparams(*sem):
    return pltpu.CompilerParams(dimension_semantics=sem, vmem_limit_bytes=VMEM_LIMIT_BYTES)


def _rms(x, gain):
    return x * lax.rsqrt(jnp.mean(x * x, axis=-1, keepdims=True) + NORM_EPS) * gain


def _nt_dot(a, b):
    return lax.dot_general(a, b, (((1,), (1,)), ((), ())), preferred_element_type=F32)


def _stack_heads(q_ref):
    return jnp.concatenate([q_ref[:, g * HEAD_DIM:(g + 1) * HEAD_DIM] for g in range(NSA_GROUP)], axis=0)


def _fill_halo_h(tiles_per_seq, x_ref, xp_ref, g_ref, h_s):
    tm = x_ref.shape[0]
    first = pl.program_id(0) % tiles_per_seq == 0
    hp = _rms(xp_ref[...], g_ref[...])
    h_s[0:CONV_HALO, :] = jnp.where(first, 0.0, hp).astype(h_s.dtype)
    h_s[CONV_HALO:CONV_HALO + tm, :] = _rms(x_ref[...], g_ref[...]).astype(h_s.dtype)


def _causal_conv(u_s, cw_ref, cb_ref, taps, tm):
    u = u_s[...]
    acc = jnp.broadcast_to(cb_ref[...], (tm, u_s.shape[1]))
    for j in range(taps):
        back = taps - 1 - j
        shifted = pltpu.roll(u, back, axis=0) if back else u
        acc = acc + shifted[CONV_HALO:CONV_HALO + tm, :] * cw_ref[j:j + 1, :]
    return acc


def _rope(x, cosf, sinf, lane):
    rot = jnp.where(lane < ROPE_DIM // 2, pltpu.roll(x, LANES - ROPE_DIM // 2, axis=1),
                    pltpu.roll(x, ROPE_DIM // 2, axis=1))
    return x * cosf + rot * sinf


def _proj_in_kernel(tiles_per_seq, x_ref, xp_ref, g_ref, w_ref, wg_ref, cw_ref, cb_ref, cos_ref, sin_ref,
                    o_ref, zg_ref, grp_ref, h_s, u_s, y_s):
    tm, tn = o_ref.shape
    j = pl.program_id(1)
    j_k, j_v, j_nq, j_nkv = Z_MK // tn, Z_MV // tn, Z_NQ // tn, Z_KC // tn

    @pl.when(j == 0)
    def _():
        _fill_halo_h(tiles_per_seq, x_ref, xp_ref, g_ref, h_s)
        zg_ref[...] = jnp.dot(h_s[CONV_HALO:CONV_HALO + tm, :], wg_ref[...], preferred_element_type=F32)
        grp_ref[...] = jnp.zeros_like(grp_ref)

    @pl.when(j < j_v)
    def _():
        u_s[...] = jnp.dot(h_s[...], w_ref[...], preferred_element_type=F32)
        y = _causal_conv(u_s, cw_ref, cb_ref, QK_CONV, tm)
        scale = jnp.where(j >= j_k, ATTN_SCALE, 1.0)
        o_ref[...] = (y * jax.nn.sigmoid(y) * scale).astype(o_ref.dtype)

    @pl.when((j >= j_v) & (j < j_nq))
    def _():
        o_ref[...] = jnp.dot(h_s[CONV_HALO:CONV_HALO + tm, :], w_ref[...],
                             preferred_element_type=F32).astype(o_ref.dtype)

    @pl.when(j >= j_nq)
    def _():
        is_q = j < j_nkv
        for r in range(tm // EPI_ROWS):
            rs = slice(r * EPI_ROWS, (r + 1) * EPI_ROWS)
            cosf, sinf = cos_ref[rs, :], sin_ref[rs, :]
            lane = lax.broadcasted_iota(jnp.int32, cosf.shape, 1)
            u = jnp.dot(h_s[CONV_HALO + r * EPI_ROWS:CONV_HALO + (r + 1) * EPI_ROWS, :], w_ref[...],
                        preferred_element_type=F32)
            for head in range(tn // HEAD_DIM):
                cs = slice(head * HEAD_DIM, (head + 1) * HEAD_DIM)
                xh = u[:, cs]
                roped = _rope(xh, cosf, sinf, lane)
                if head < NSA_KV_HEADS:
                    yh = roped * jnp.where(is_q, ATTN_SCALE, 1.0)
                else:
                    yh = jnp.where(is_q, roped * ATTN_SCALE, xh)
                o_ref[rs, cs] = yh.astype(o_ref.dtype)
                y_s[head, rs, :] = yh

        @pl.when(j == j_nkv)
        def _():
            ngrp = tm // CMP_STRIDE
            for head in range(tn // HEAD_DIM):
                for l in range(CMP_STRIDE):
                    col = (head * CMP_STRIDE + l) * HEAD_DIM
                    grp_ref[:, col:col + HEAD_DIM] = y_s[head, pl.ds(l, ngrp, stride=CMP_STRIDE),
                                                         :].astype(grp_ref.dtype)


def proj_in(x, g, w_z, w_gate, conv_w, conv_b, cosf, sinf, seq):
    m, k = x.shape
    tm, tn, halo = MM_TM, MM_TN, CONV_HALO
    assert tn == 2 * NSA_KV_W and Z_MV % tn == 0 and Z_NQ % tn == 0 and Z_KC % tn == 0
    n_conv = Z_MV // tn
    nper = seq // tm
    conv_tile = lambda j: jnp.minimum(j, n_conv - 1)
    return pl.pallas_call(
        functools.partial(_proj_in_kernel, seq // tm),
        out_shape=(jax.ShapeDtypeStruct((m, Z_W), BF16), jax.ShapeDtypeStruct((m, LANES), F32),
                   jax.ShapeDtypeStruct((m // CMP_STRIDE, tn * CMP_STRIDE), BF16)),
        grid=(m // tm, Z_W // tn),
        in_specs=[
            pl.BlockSpec((tm, k), lambda i, j: (i, 0)),
            pl.BlockSpec((halo, k), lambda i, j: (jnp.maximum(i * (tm // halo) - 1, 0), 0)),
            pl.BlockSpec((1, k), lambda i, j: (0, 0)),
            pl.BlockSpec((k, tn), lambda i, j: (0, j)),
            pl.BlockSpec((k, LANES), lambda i, j: (0, 0)),
            pl.BlockSpec((QK_CONV, tn), lambda i, j: (0, conv_tile(j))),
            pl.BlockSpec((1, tn), lambda i, j: (0, conv_tile(j))),
            pl.BlockSpec((tm, LANES), lambda i, j: (i % nper, 0)),
            pl.BlockSpec((tm, LANES), lambda i, j: (i % nper, 0)),
        ],
        out_specs=(pl.BlockSpec((tm, tn), lambda i, j: (i, j)),
                   pl.BlockSpec((tm, LANES), lambda i, j: (i, 0)),
                   pl.BlockSpec((tm // CMP_STRIDE, tn * CMP_STRIDE), lambda i, j: (i, 0))),
        scratch_shapes=[pltpu.VMEM((tm + halo, k), BF16), pltpu.VMEM((tm + halo, tn), F32),
                        pltpu.VMEM((tn // HEAD_DIM, tm, HEAD_DIM), F32)],
        compiler_params=_cparams("parallel", "arbitrary"),
        name="proj_in",
    )(x, x, g.reshape(1, k), w_z, w_gate, conv_w, conv_b.reshape(1, -1), cosf, sinf)


def _matmul_res_kernel(n_lhs, *refs):
    lhs = refs[:n_lhs]
    ws = refs[n_lhs:2 * n_lhs]
    res_ref, o_ref = refs[2 * n_lhs], refs[2 * n_lhs + 1]
    acc = res_ref[...]
    for a_ref, w_ref in zip(lhs, ws):
        acc = acc + jnp.dot(a_ref[...], w_ref[...], preferred_element_type=F32)
    o_ref[...] = acc


def matmul_res(lhs_list, w_list, res, tm, tn):
    m, n = res.shape
    n_lhs = len(lhs_list)
    in_specs = [pl.BlockSpec((tm, a.shape[1]), lambda i, j: (i, 0)) for a in lhs_list]
    in_specs += [pl.BlockSpec((w.shape[0], tn), lambda i, j: (0, j)) for w in w_list]
    in_specs += [pl.BlockSpec((tm, tn), lambda i, j: (i, j))]
    return pl.pallas_call(
        functools.partial(_matmul_res_kernel, n_lhs),
        out_shape=jax.ShapeDtypeStruct((m, n), F32),
        grid=(m // tm, n // tn),
        in_specs=in_specs,
        out_specs=pl.BlockSpec((tm, tn), lambda i, j: (i, j)),
        compiler_params=_cparams("parallel", "arbitrary"),
        name="matmul_res",
    )(*lhs_list, *w_list, res)


def _mlstm_kernel(seq, q_ref, k_ref, v_ref, zo_ref, gi_ref, gf_ref, bi_ref, bf_ref, nrm_ref, o_ref,
                  b_s, i_s, cx_s, m_s):
    L = MLSTM_CHUNK
    hps = MLSTM_HEADS_PER_STEP
    nchunks = seq // L

    lane = lax.broadcasted_iota(jnp.int32, (nchunks, L), 1)
    for hh in range(hps):
        i_s[hh] = gi_ref[0, hh] + bi_ref[hh]
        fpre = gf_ref[0, hh] + bf_ref[hh]
        csum = jnp.minimum(fpre, 0.0) - jnp.log1p(jnp.exp(-jnp.abs(fpre)))
        for sh in (1 << e for e in range(L.bit_length() - 1)):
            csum = csum + jnp.where(lane >= sh, pltpu.roll(csum, sh, axis=1), 0.0)
        b_s[hh] = csum

    cx_s[...] = jnp.zeros_like(cx_s)
    m_s[...] = jnp.zeros_like(m_s)

    row = lax.broadcasted_iota(jnp.int32, (L, L), 0)
    col = lax.broadcasted_iota(jnp.int32, (L, L), 1)
    eye = row == col
    causal = col <= row
    ones = jnp.ones((L, LANES), BF16)

    def chunk_body(c, carry):
        r0 = pl.multiple_of(c * L, L)
        for hh in range(hps):
            cs = slice(hh * HEAD_DIM, (hh + 1) * HEAD_DIM)
            qc = q_ref[pl.ds(r0, L), cs]
            kc = k_ref[pl.ds(r0, L), cs]
            vx = jnp.concatenate([v_ref[pl.ds(r0, L), cs], ones], axis=1)
            b_row = b_s[hh, pl.ds(c, 1), :]
            u_row = i_s[hh, pl.ds(c, 1), :] - b_row
            b_col = jnp.sum(jnp.where(eye, b_row, 0.0), axis=-1, keepdims=True)
            m_prev = m_s[hh]
            cx = cx_s[hh]

            umat = jnp.where(causal, u_row, -jnp.inf)
            m_col = jnp.maximum(m_prev, jnp.max(umat, axis=-1, keepdims=True))
            w_inter = jnp.exp(m_prev - m_col)
            s_qk = _nt_dot(qc, kc) * jnp.exp(umat - m_col)
            nd = (w_inter * jnp.dot(qc, cx.astype(BF16), preferred_element_type=F32)
                  + jnp.dot(s_qk.astype(BF16), vx, preferred_element_type=F32))
            floor = jnp.exp(-(b_col + m_col))
            h = nd[:, :HEAD_DIM] / jnp.maximum(jnp.abs(nd[:, HEAD_DIM:]), floor)

            m_last = m_col[L - 1:L, :]
            kw_t = kc.astype(F32).T * jnp.exp(u_row - m_last)
            cx_s[hh] = (jnp.exp(m_prev - m_last) * cx
                        + jnp.dot(kw_t.astype(BF16), vx, preferred_element_type=F32))
            m_s[hh] = b_row[:, L - 1:L] + m_last

            hn = _rms(h, nrm_ref[:, cs])
            gate = jax.nn.sigmoid(zo_ref[pl.ds(r0, L), cs].astype(F32))
            o_ref[pl.ds(r0, L), cs] = (hn * gate).astype(o_ref.dtype)
        return carry

    lax.fori_loop(0, nchunks, chunk_body, 0)


def mlstm_group(z, gi, gf, bi, bf, norm_g, batch, seq):
    hps = MLSTM_HEADS_PER_STEP
    wblk = hps * HEAD_DIM
    nchunks = seq // MLSTM_CHUNK
    nhb = MLSTM_HEADS // hps
    zspec = lambda col0: pl.BlockSpec((seq, wblk), lambda b, h: (b, col0 // wblk + h))
    gspec = pl.BlockSpec((1, hps, nchunks, MLSTM_CHUNK), lambda b, h: (b, h, 0, 0))
    bspec = pl.BlockSpec((hps, 1, MLSTM_CHUNK), lambda b, h: (h, 0, 0))
    return pl.pallas_call(
        functools.partial(_mlstm_kernel, seq),
        out_shape=jax.ShapeDtypeStruct((batch * seq, MLSTM_W), BF16),
        grid=(batch, nhb),
        in_specs=[zspec(Z_MQ), zspec(Z_MK), zspec(Z_MV), zspec(Z_MO), gspec, gspec, bspec, bspec,
                  pl.BlockSpec((1, wblk), lambda b, h: (0, h))],
        out_specs=pl.BlockSpec((seq, wblk), lambda b, h: (b, h)),
        scratch_shapes=[
            pltpu.VMEM((hps, nchunks, MLSTM_CHUNK), F32),
            pltpu.VMEM((hps, nchunks, MLSTM_CHUNK), F32),
            pltpu.VMEM((hps, HEAD_DIM, 2 * HEAD_DIM), F32),
            pltpu.VMEM((hps, 1, 1), F32),
        ],
        compiler_params=_cparams("parallel", "parallel"),
        name="mlstm_group",
    )(z, z, z, z, gi, gf, bi, bf, norm_g.reshape(1, -1))


def _compress_kernel(x_ref, pe_ref, w1_ref, w2_ref, o_ref, sh_s):
    half = CMP_STRIDE * HEAD_DIM
    ngroups = x_ref.shape[0]
    x = x_ref[...].astype(F32)
    lo = jnp.dot((x + pe_ref[:, :half]).astype(BF16), w1_ref[:half, :], preferred_element_type=F32)
    hi = jnp.dot((x + pe_ref[:, half:]).astype(BF16), w1_ref[half:, :], preferred_element_type=F32)
    sh_s[0:ngroups, :] = hi
    sh_s[ngroups:ngroups + SUBLANES_F32, :] = jnp.zeros((SUBLANES_F32, CMP_HIDDEN), F32)
    pre = lo + sh_s[1:ngroups + 1, :]
    act = jax.nn.gelu(pre, approximate=True)
    o_ref[0] = jnp.dot(act.astype(BF16), w2_ref[...], preferred_element_type=F32).astype(o_ref.dtype)


def compress(grp, first_head, pe, w1, w2, batch, seq):
    nslab, ngroups, gw = batch * NSA_KV_HEADS, seq // CMP_STRIDE, CMP_STRIDE * HEAD_DIM
    return pl.pallas_call(
        _compress_kernel,
        out_shape=jax.ShapeDtypeStruct((nslab, ngroups, HEAD_DIM), BF16),
        grid=(nslab,),
        in_specs=[pl.BlockSpec((ngroups, gw), lambda s: (s // NSA_KV_HEADS, first_head + s % NSA_KV_HEADS)),
                  pl.BlockSpec((1, 2 * gw), lambda s: (0, 0)),
                  pl.BlockSpec((2 * gw, CMP_HIDDEN), lambda s: (0, 0)),
                  pl.BlockSpec((CMP_HIDDEN, HEAD_DIM), lambda s: (0, 0))],
        out_specs=pl.BlockSpec((1, ngroups, HEAD_DIM), lambda s: (s, 0, 0)),
        scratch_shapes=[pltpu.VMEM((ngroups + SUBLANES_F32, CMP_HIDDEN), F32)],
        compiler_params=_cparams("parallel"),
        name="nsa_compress",
    )(grp, pe.reshape(1, -1), w1, w2)


def _cmp_select_kernel(nsel, q_ref, kc_ref, vc_ref, ov_ref, o_ref, sel_ref):
    tq = q_ref.shape[0]
    s0 = pl.program_id(2) * tq
    kc = kc_ref[0]
    vc = vc_ref[0]
    ncp = kc.shape[0]
    pos = s0 + lax.broadcasted_iota(jnp.int32, (tq, ncp), 0)
    cidx = lax.broadcasted_iota(jnp.int32, (tq, ncp), 1)
    valid = cidx * CMP_STRIDE + (CMP_BLOCK - 1) <= pos
    any_valid = s0 + lax.broadcasted_iota(jnp.int32, (tq, 1), 0) >= CMP_BLOCK - 1
    sc = _nt_dot(_stack_heads(q_ref), kc)
    sc = jnp.where(valid[None], sc.reshape(NSA_GROUP, tq, ncp), NEG)
    e = jnp.where(valid[None], jnp.exp(sc - jnp.max(sc, axis=-1, keepdims=True)), 0.0)
    denom = jnp.sum(e, axis=-1, keepdims=True)
    p = e / jnp.where(any_valid[None], denom, 1.0)
    o = jnp.dot(p.reshape(NSA_GROUP * tq, ncp).astype(BF16), vc, preferred_element_type=F32)
    p_sum = p[0]
    for g in range(NSA_GROUP):
        o_ref[:, g * HEAD_DIM:(g + 1) * HEAD_DIM] = o[g * tq:(g + 1) * tq, :].astype(o_ref.dtype)
        if g:
            p_sum = p_sum + p[g]
    imp = jnp.dot(p_sum, ov_ref[...], preferred_element_type=F32, precision=lax.Precision.HIGHEST)
    jb = cidx
    cur = pos // SEL_BLOCK
    forced = (jb == 0) | (jb == cur) | (jb == cur - 1)
    imp = jnp.where(forced, FORCE_SCORE, imp)
    imp = jnp.where(jb <= cur, imp, -1.0)
    imp_t = imp.T[:nsel, :]
    jb_t = lax.broadcasted_iota(jnp.int32, (nsel, tq), 0)
    rank = jnp.zeros((nsel, tq), F32)
    for j2 in range(nsel):
        rowv = imp_t[j2:j2 + 1, :]
        before = (rowv > imp_t) | ((rowv == imp_t) & (jb_t > j2))
        rank = rank + jnp.where(before, 1.0, 0.0)
    sel_t = jnp.where(rank < float(SEL_TOPK), 1.0, 0.0)
    sel_t = jnp.concatenate([sel_t, jnp.zeros((ncp - nsel, tq), F32)], axis=0)
    sel_ref[...] = sel_t.T.astype(sel_ref.dtype)


def cmp_select(z, kcmp, vcmp, overlap, batch, seq):
    m = z.shape[0]
    tq = CMP_TQ
    nq = seq // tq
    ncp = kcmp.shape[1]
    gw = NSA_GROUP * HEAD_DIM
    return pl.pallas_call(
        functools.partial(_cmp_select_kernel, seq // SEL_BLOCK),
        out_shape=(jax.ShapeDtypeStruct((m, NSA_W), BF16),
                   jax.ShapeDtypeStruct((m, NSA_KV_HEADS * LANES), BF16)),
        grid=(batch, NSA_KV_HEADS, nq),
        in_specs=[pl.BlockSpec((tq, gw), lambda b, h, i: (b * nq + i, Z_NQ // gw + h)),
                  pl.BlockSpec((1, ncp, HEAD_DIM), lambda b, h, i: (b * NSA_KV_HEADS + h, 0, 0)),
                  pl.BlockSpec((1, ncp, HEAD_DIM), lambda b, h, i: (b * NSA_KV_HEADS + h, 0, 0)),
                  pl.BlockSpec((ncp, LANES), lambda b, h, i: (0, 0))],
        out_specs=(pl.BlockSpec((tq, gw), lambda b, h, i: (b * nq + i, h)),
                   pl.BlockSpec((tq, LANES), lambda b, h, i: (b * nq + i, h))),
        compiler_params=_cparams("parallel", "parallel", "parallel"),
        name="nsa_cmp_select",
    )(z, kcmp, vcmp, overlap)


def _sel_attn_kernel(q_ref, k_ref, v_ref, oh_ref, sel_ref, o_ref, qx_s, m_s, acc_s, s0_s, s1_s):
    tq, tk = ATTN_TQ, ATTN_TK
    i = pl.program_id(2)
    penalty = ((sel_ref[...].astype(F32) - 1.0) * (-NEG)).astype(BF16)
    for g in range(NSA_GROUP):
        cs = slice(g * HEAD_DIM, (g + 1) * HEAD_DIM)
        qx_s[g * tq:(g + 1) * tq, :] = jnp.concatenate([q_ref[:, cs], penalty], axis=1)
    m_s[...] = jnp.full_like(m_s, NEG)
    acc_s[...] = jnp.zeros_like(acc_s)
    ones = jnp.ones((tk, LANES), BF16)

    def scores(t, s_buf):
        k0 = pl.multiple_of(t * tk, tk)
        kx = jnp.concatenate([k_ref[pl.ds(k0, tk), :], oh_ref[pl.ds(k0, tk), :]], axis=1)
        s_buf[...] = _nt_dot(qx_s[...], kx)

    def update(t, s_buf, diagonal):
        k0 = pl.multiple_of(t * tk, tk)
        vx = jnp.concatenate([v_ref[pl.ds(k0, tk), :], ones], axis=1)
        s = s_buf[...]
        if diagonal:
            row = lax.broadcasted_iota(jnp.int32, (tq, tk), 0)
            col = lax.broadcasted_iota(jnp.int32, (tq, tk), 1)
            causal = jnp.where(col <= row, 0.0, NEG)
            s = (s.reshape(NSA_GROUP, tq, tk) + causal[None]).reshape(NSA_GROUP * tq, tk)
        m_old = m_s[...]
        m_new = jnp.maximum(m_old, jnp.max(s, axis=-1, keepdims=True))
        alpha = jnp.exp(m_old - m_new)
        p = jnp.exp((s - jnp.concatenate([m_new] * (tk // LANES), axis=1)).astype(BF16))
        acc_s[...] = (jnp.concatenate([alpha, alpha], axis=1) * acc_s[...]
                      + jnp.dot(p, vx, preferred_element_type=F32))
        m_s[...] = m_new

    scores(0, s0_s)

    def pair(p, carry):
        t = 2 * p
        scores(t + 1, s1_s)
        update(t, s0_s, False)
        scores(t + 2, s0_s)
        update(t + 1, s1_s, False)
        return carry

    lax.fori_loop(0, i // 2, pair, 0)

    @pl.when(i % 2 == 0)
    def _():
        update(i, s0_s, True)

    @pl.when(i % 2 == 1)
    def _():
        scores(i, s1_s)
        update(i - 1, s0_s, False)
        update(i, s1_s, True)
    for g in range(NSA_GROUP):
        cs = slice(g * HEAD_DIM, (g + 1) * HEAD_DIM)
        acc = acc_s[g * tq:(g + 1) * tq, :]
        o_ref[:, cs] = (acc[:, :HEAD_DIM] / acc[:, HEAD_DIM:]).astype(o_ref.dtype)


def _kv_spec(seq, col0):
    return pl.BlockSpec((seq, HEAD_DIM), lambda b, h, i: (b, col0 // HEAD_DIM + h))


def sel_attn(z, onehot, sel, batch, seq):
    assert ATTN_TQ == ATTN_TK
    m = z.shape[0]
    tq = ATTN_TQ
    nq = seq // tq
    gw = NSA_GROUP * HEAD_DIM
    return pl.pallas_call(
        _sel_attn_kernel,
        out_shape=jax.ShapeDtypeStruct((m, NSA_W), BF16),
        grid=(batch, NSA_KV_HEADS, nq),
        in_specs=[pl.BlockSpec((tq, gw), lambda b, h, i: (b * nq + i, Z_NQ // gw + h)),
                  _kv_spec(seq, Z_KS), _kv_spec(seq, Z_VS),
                  pl.BlockSpec((seq, LANES), lambda b, h, i: (0, 0)),
                  pl.BlockSpec((tq, LANES), lambda b, h, i: (b * nq + i, h))],
        out_specs=pl.BlockSpec((tq, gw), lambda b, h, i: (b * nq + i, h)),
        scratch_shapes=[pltpu.VMEM((NSA_GROUP * tq, 2 * LANES), BF16),
                        pltpu.VMEM((NSA_GROUP * tq, LANES), F32),
                        pltpu.VMEM((NSA_GROUP * tq, 2 * LANES), F32),
                        pltpu.VMEM((NSA_GROUP * tq, ATTN_TK), F32),
                        pltpu.VMEM((NSA_GROUP * tq, ATTN_TK), F32)],
        compiler_params=_cparams("parallel", "parallel", "arbitrary"),
        name="nsa_sel_attn",
    )(z, z, z, onehot, sel)


def _win_attn_kernel(q_ref, k_ref, v_ref, o_ref):
    tq, nk = ATTN_TQ, WIN_KEYS
    s0 = pl.program_id(2) * tq
    start = pl.multiple_of(jnp.maximum(s0 - WINDOW, 0), tq)
    kw = k_ref[pl.ds(start, nk), :]
    vx = jnp.concatenate([v_ref[pl.ds(start, nk), :], jnp.ones((nk, LANES), BF16)], axis=1)
    qpos = s0 + lax.broadcasted_iota(jnp.int32, (tq, nk), 0)
    kpos = start + lax.broadcasted_iota(jnp.int32, (tq, nk), 1)
    diff = qpos - kpos
    bias = jnp.where((diff >= 0) & (diff < WINDOW), 0.0, NEG)
    for g in range(NSA_GROUP):
        cs = slice(g * HEAD_DIM, (g + 1) * HEAD_DIM)
        s = _nt_dot(q_ref[:, cs], kw) + bias
        p = jnp.exp((s - jnp.max(s, axis=-1, keepdims=True)).astype(BF16))
        a = jnp.dot(p, vx, preferred_element_type=F32)
        o_ref[:, cs] = (a[:, :HEAD_DIM] / a[:, HEAD_DIM:]).astype(o_ref.dtype)


def win_attn(z, batch, seq):
    m = z.shape[0]
    tq = ATTN_TQ
    nq = seq // tq
    gw = NSA_GROUP * HEAD_DIM
    return pl.pallas_call(
        _win_attn_kernel,
        out_shape=jax.ShapeDtypeStruct((m, NSA_W), BF16),
        grid=(batch, NSA_KV_HEADS, nq),
        in_specs=[pl.BlockSpec((tq, gw), lambda b, h, i: (b * nq + i, Z_NQ // gw + h)),
                  _kv_spec(seq, Z_KW), _kv_spec(seq, Z_VW)],
        out_specs=pl.BlockSpec((tq, gw), lambda b, h, i: (b * nq + i, h)),
        compiler_params=_cparams("parallel", "parallel", "parallel"),
        name="nsa_win_attn",
    )(z, z, z)


def _nsa_combine_kernel(oc_ref, os_ref, ow_ref, zg_ref, ex_ref, nrm_ref, o_ref, mix_s):
    tm = oc_ref.shape[0]
    gates = jax.nn.sigmoid(zg_ref[...])
    g_hi = gates.astype(BF16)
    g_lo = (gates - g_hi.astype(F32)).astype(BF16)
    mix = None
    for br, src in enumerate((oc_ref, os_ref, ow_ref)):
        spread = (jnp.dot(g_hi, ex_ref[br], preferred_element_type=F32)
                  + jnp.dot(g_lo, ex_ref[br], preferred_element_type=F32))
        term = spread * src[...].astype(F32)
        mix = term if mix is None else mix + term
    mix_s[...] = mix
    inv = lax.rsqrt(jnp.mean(mix * mix, axis=-1, keepdims=True) + NORM_EPS)
    o_ref[...] = (mix_s[...] * inv * nrm_ref[...]).astype(o_ref.dtype)


def _gate_spread():
    col = jnp.arange(LANES)[None, :, None]
    head = (jnp.arange(NSA_W) // HEAD_DIM)[None, None, :]
    br = jnp.arange(3)[:, None, None]
    return (col == ZG_NG + 3 * head + br).astype(BF16)


def nsa_combine(o_cmp, o_slc, o_win, zg, norm_g):
    m = o_cmp.shape[0]
    tm = ROW_TM
    ospec = pl.BlockSpec((tm, NSA_W), lambda i: (i, 0))
    return pl.pallas_call(
        _nsa_combine_kernel,
        out_shape=jax.ShapeDtypeStruct((m, NSA_W), BF16),
        grid=(m // tm,),
        in_specs=[ospec, ospec, ospec, pl.BlockSpec((tm, LANES), lambda i: (i, 0)),
                  pl.BlockSpec((3, LANES, NSA_W), lambda i: (0, 0, 0)),
                  pl.BlockSpec((1, NSA_W), lambda i: (0, 0))],
        out_specs=ospec,
        scratch_shapes=[pltpu.VMEM((tm, NSA_W), F32)],
        compiler_params=_cparams("parallel"),
        name="nsa_combine",
    )(o_cmp, o_slc, o_win, zg, _gate_spread(), norm_g.reshape(1, -1))


def _ffn_up_kernel(tiles_per_seq, x_ref, xp_ref, g_ref, wg_ref, wu_ref, cwg_ref, cwu_ref,
                   cbg_ref, cbu_ref, o_ref, h_s, ug_s, uu_s):
    tm = x_ref.shape[0]

    @pl.when(pl.program_id(1) == 0)
    def _():
        _fill_halo_h(tiles_per_seq, x_ref, xp_ref, g_ref, h_s)

    ug_s[...] = jnp.dot(h_s[...], wg_ref[...], preferred_element_type=F32)
    uu_s[...] = jnp.dot(h_s[...], wu_ref[...], preferred_element_type=F32)
    gate = _causal_conv(ug_s, cwg_ref, cbg_ref, FFN_CONV, tm)
    up = _causal_conv(uu_s, cwu_ref, cbu_ref, FFN_CONV, tm)
    o_ref[...] = (gate * jax.nn.sigmoid(gate) * up).astype(o_ref.dtype)


def ffn_up(x, g, w_up, conv_w, conv_b, seq):
    m, k = x.shape
    tm, tn, halo = MM_TM, MM_TN, CONV_HALO
    ntile = D_FF // tn
    cb = conv_b.reshape(1, -1)
    return pl.pallas_call(
        functools.partial(_ffn_up_kernel, seq // tm),
        out_shape=jax.ShapeDtypeStruct((m, D_FF), BF16),
        grid=(m // tm, ntile),
        in_specs=[
            pl.BlockSpec((tm, k), lambda i, j: (i, 0)),
            pl.BlockSpec((halo, k), lambda i, j: (jnp.maximum(i * (tm // halo) - 1, 0), 0)),
            pl.BlockSpec((1, k), lambda i, j: (0, 0)),
            pl.BlockSpec((k, tn), lambda i, j: (0, j)),
            pl.BlockSpec((k, tn), lambda i, j: (0, ntile + j)),
            pl.BlockSpec((FFN_CONV, tn), lambda i, j: (0, j)),
            pl.BlockSpec((FFN_CONV, tn), lambda i, j: (0, ntile + j)),
            pl.BlockSpec((1, tn), lambda i, j: (0, j)),
            pl.BlockSpec((1, tn), lambda i, j: (0, ntile + j)),
        ],
        out_specs=pl.BlockSpec((tm, tn), lambda i, j: (i, j)),
        scratch_shapes=[pltpu.VMEM((tm + halo, k), BF16),
                        pltpu.VMEM((tm + halo, tn), F32),
                        pltpu.VMEM((tm + halo, tn), F32)],
        compiler_params=_cparams("parallel", "arbitrary"),
        name="ffn_up",
    )(x, x, g.reshape(1, k), w_up, w_up, conv_w, conv_w, cb, cb)


def _rmsnorm_kernel(x_ref, g_ref, o_ref):
    o_ref[...] = _rms(x_ref[...], g_ref[...])


def rmsnorm(x, g):
    m, k = x.shape
    tm = ROW_TM
    return pl.pallas_call(
        _rmsnorm_kernel,
        out_shape=jax.ShapeDtypeStruct((m, k), F32),
        grid=(m // tm,),
        in_specs=[pl.BlockSpec((tm, k), lambda i: (i, 0)), pl.BlockSpec((1, k), lambda i: (0, 0))],
        out_specs=pl.BlockSpec((tm, k), lambda i: (i, 0)),
        compiler_params=_cparams("parallel"),
        name="final_rmsnorm",
    )(x, g.reshape(1, k))


def _gate_weight(w_in):
    gates = jnp.concatenate([w_in[:, W_IN_MI:W_IN_MI + 2 * MLSTM_HEADS],
                             w_in[:, W_IN_NG:W_IN_NG + 3 * NSA_HEADS]], axis=1)
    return jnp.pad(gates, ((0, 0), (0, LANES - gates.shape[1]))).astype(BF16)


def _rope_tables(seq):
    pos = jnp.arange(seq, dtype=F32)
    inv = ROPE_THETA ** (-jnp.arange(0, ROPE_DIM, 2, dtype=F32) / ROPE_DIM)
    ang = pos[:, None] * inv[None, :]
    cos, sin = jnp.cos(ang), jnp.sin(ang)
    rest = HEAD_DIM - ROPE_DIM
    cosf = jnp.concatenate([cos, cos, jnp.ones((seq, rest), F32)], axis=1)
    sinf = jnp.concatenate([-sin, sin, jnp.zeros((seq, rest), F32)], axis=1)
    return cosf, sinf


def _overlap_matrix(seq, ncmp_pad):
    nsel = seq // SEL_BLOCK
    cstart = jnp.arange(ncmp_pad) * CMP_STRIDE
    jb = jnp.arange(LANES)
    ov = ((cstart[:, None] < (jb[None, :] + 1) * SEL_BLOCK)
          & (cstart[:, None] + CMP_BLOCK > jb[None, :] * SEL_BLOCK) & (jb[None, :] < nsel))
    return ov.astype(F32)


def _block_onehot(seq):
    blk = jnp.arange(seq)[:, None] // SEL_BLOCK
    return (blk == jnp.arange(LANES)[None, :]).astype(BF16)


def _layer(x, p, cosf, sinf, overlap, onehot, batch, seq):
    w_in = p["w_in"]
    w_z = jnp.concatenate([w_in[:, :W_IN_MI], w_in[:, W_IN_NSA[0]:W_IN_NSA[1]]], axis=1).astype(BF16)
    z, zg, grp = proj_in(x, p["attn_norm"], w_z, _gate_weight(w_in), p["qk_conv_w"], p["qk_conv_b"],
                         cosf, sinf, seq)

    nchunks = seq // MLSTM_CHUNK
    gate_rows = lambda c0: (zg[:, c0:c0 + MLSTM_HEADS].reshape(batch, seq, MLSTM_HEADS)
                            .transpose(0, 2, 1).reshape(batch, MLSTM_HEADS, nchunks, MLSTM_CHUNK))
    lanes = lambda v: jnp.broadcast_to(v.reshape(MLSTM_HEADS, 1, 1), (MLSTM_HEADS, 1, MLSTM_CHUNK))
    mix_a = mlstm_group(z, gate_rows(ZG_MI), gate_rows(ZG_MF), lanes(p["i_bias"]),
                        lanes(p["f_bias"]), p["mlstm_norm"], batch, seq)

    kcmp = compress(grp, 0, p["cmp_pe_k"], p["cmp_w1_k"].astype(BF16), p["cmp_w2_k"].astype(BF16),
                    batch, seq)
    vcmp = compress(grp, NSA_KV_HEADS, p["cmp_pe_v"], p["cmp_w1_v"].astype(BF16),
                    p["cmp_w2_v"].astype(BF16), batch, seq)
    o_cmp, sel = cmp_select(z, kcmp, vcmp, overlap, batch, seq)
    o_slc = sel_attn(z, onehot, sel, batch, seq)
    o_win = win_attn(z, batch, seq)
    mix_b = nsa_combine(o_cmp, o_slc, o_win, zg, p["nsa_norm"])

    w_out = p["w_out"].astype(BF16)
    x = matmul_res([mix_a, mix_b], [w_out[:MLSTM_W], w_out[MLSTM_W:]], x, OUT_TM, D_MODEL)

    act = ffn_up(x, p["ffn_norm"], p["w_up"].astype(BF16), p["ffn_conv_w"], p["ffn_conv_b"], seq)
    x = matmul_res([act], [p["w_down"].astype(BF16)], x, MM_TM, DOWN_TN)
    return x


_LAYER_PARAMS = ("attn_norm", "w_in", "qk_conv_w", "qk_conv_b", "i_bias", "f_bias", "mlstm_norm",
                 "cmp_pe_k", "cmp_pe_v", "cmp_w1_k", "cmp_w2_k", "cmp_w1_v", "cmp_w2_v", "nsa_norm",
                 "w_out", "ffn_norm", "w_up", "ffn_conv_w", "ffn_conv_b", "w_down")


def kernel(x, attn_norm, w_in, qk_conv_w, qk_conv_b, i_bias, f_bias, mlstm_norm, cmp_pe_k, cmp_pe_v,
           cmp_w1_k, cmp_w2_k, cmp_w1_v, cmp_w2_v, nsa_norm, w_out, ffn_norm, w_up, ffn_conv_w,
           ffn_conv_b, w_down, final_norm):
    stacked = dict(zip(_LAYER_PARAMS, (attn_norm, w_in, qk_conv_w, qk_conv_b, i_bias, f_bias,
                                       mlstm_norm, cmp_pe_k, cmp_pe_v, cmp_w1_k, cmp_w2_k, cmp_w1_v,
                                       cmp_w2_v, nsa_norm, w_out, ffn_norm, w_up, ffn_conv_w,
                                       ffn_conv_b, w_down)))
    batch, seq, d = x.shape
    assert d == D_MODEL and seq % MM_TM == 0 and seq % ATTN_TQ == 0 and seq >= WIN_KEYS
    assert seq // CMP_STRIDE == LANES and seq // SEL_BLOCK <= LANES
    cosf, sinf = _rope_tables(seq)
    overlap = _overlap_matrix(seq, seq // CMP_STRIDE)
    onehot = _block_onehot(seq)
    xf = x.reshape(batch * seq, d)
    for layer in range(DEPTH):
        p = {k: v[layer] for k, v in stacked.items()}
        xf = _layer(xf, p, cosf, sinf, overlap, onehot, batch, seq)
    return rmsnorm(xf, final_norm).reshape(batch, seq, d)
```

```python
import functools

import jax
import jax.numpy as jnp
from jax import lax
from jax.experimental import pallas as pl
from jax.experimental.pallas import tpu as pltpu

D_MODEL = 2048
DEPTH = 2
MLSTM_HEADS = 8
HEAD_DIM = 128
MLSTM_W = MLSTM_HEADS * HEAD_DIM
QK_CONV = 4
NSA_HEADS = 8
NSA_KV_HEADS = 2
NSA_GROUP = NSA_HEADS // NSA_KV_HEADS
NSA_W = NSA_HEADS * HEAD_DIM
NSA_KV_W = NSA_KV_HEADS * HEAD_DIM
CMP_BLOCK = 32
CMP_STRIDE = 16
CMP_HIDDEN = 256
SEL_BLOCK = 64
SEL_TOPK = 16
WINDOW = 512
ROPE_THETA = 500000.0
ROPE_DIM = HEAD_DIM // 4
D_FF = 5632
FFN_CONV = 3
NORM_EPS = 1e-6
NEG = -1e30
FORCE_SCORE = 1e4
ATTN_SCALE = HEAD_DIM ** -0.5

LANES = 128
SUBLANES_F32 = 8
SUBLANES_BF16 = 16
VMEM_LIMIT_BYTES = 48 * 1024 * 1024

MM_TM = 1024
MM_TN = 512
EPI_ROWS = 128
DOWN_TN = 512
OUT_TM = 512
MLSTM_CHUNK = 256
MLSTM_HEADS_PER_STEP = 4
ATTN_TQ = 256
CMP_TQ = 512
ATTN_TK = 256
WIN_KEYS = WINDOW + ATTN_TQ
ROW_TM = 512
CONV_HALO = SUBLANES_BF16

BF16 = jnp.bfloat16
F32 = jnp.float32

W_IN_MI = 4 * MLSTM_W
W_IN_NSA = (W_IN_MI + 2 * MLSTM_HEADS, W_IN_MI + 2 * MLSTM_HEADS + NSA_W + 6 * NSA_KV_W)
W_IN_NG = W_IN_NSA[1]
Z_MQ, Z_MK, Z_MV, Z_MO, Z_NQ = (i * MLSTM_W for i in range(5))
Z_KC, Z_VC, Z_KS, Z_VS, Z_KW, Z_VW = (Z_NQ + NSA_W + i * NSA_KV_W for i in range(6))
Z_W = Z_VW + NSA_KV_W
ZG_MI, ZG_MF, ZG_NG = 0, MLSTM_HEADS, 2 * MLSTM_HEADS


def _cparams(*sem):
    return pltpu.CompilerParams(dimension_semantics=sem, vmem_limit_bytes=VMEM_LIMIT_BYTES)


def _rms(x, gain):
    return x * lax.rsqrt(jnp.mean(x * x, axis=-1, keepdims=True) + NORM_EPS) * gain


def _nt_dot(a, b):
    return lax.dot_general(a, b, (((1,), (1,)), ((), ())), preferred_element_type=F32)


def _stack_heads(q_ref):
    return jnp.concatenate([q_ref[:, g * HEAD_DIM:(g + 1) * HEAD_DIM] for g in range(NSA_GROUP)], axis=0)


def _fill_halo_h(tiles_per_seq, x_ref, xp_ref, g_ref, h_s):
    tm = x_ref.shape[0]
    first = pl.program_id(0) % tiles_per_seq == 0
    hp = _rms(xp_ref[...], g_ref[...])
    h_s[0:CONV_HALO, :] = jnp.where(first, 0.0, hp).astype(h_s.dtype)
    h_s[CONV_HALO:CONV_HALO + tm, :] = _rms(x_ref[...], g_ref[...]).astype(h_s.dtype)


def _causal_conv(u_s, cw_ref, cb_ref, taps, tm):
    u = u_s[...]
    acc = jnp.broadcast_to(cb_ref[...], (tm, u_s.shape[1]))
    for j in range(taps):
        back = taps - 1 - j
        shifted = pltpu.roll(u, back, axis=0) if back else u
        acc = acc + shifted[CONV_HALO:CONV_HALO + tm, :] * cw_ref[j:j + 1, :]
    return acc


def _rope(x, cosf, sinf, lane):
    rot = jnp.where(lane < ROPE_DIM // 2, pltpu.roll(x, LANES - ROPE_DIM // 2, axis=1),
                    pltpu.roll(x, ROPE_DIM // 2, axis=1))
    return x * cosf + rot * sinf


def _proj_in_kernel(tiles_per_seq, x_ref, xp_ref, g_ref, w_ref, wg_ref, cw_ref, cb_ref, cos_ref, sin_ref,
                    o_ref, zg_ref, grp_ref, h_s, u_s, y_s):
    tm, tn = o_ref.shape
    j = pl.program_id(1)
    j_k, j_v, j_nq, j_nkv = Z_MK // tn, Z_MV // tn, Z_NQ // tn, Z_KC // tn

    @pl.when(j == 0)
    def _():
        _fill_halo_h(tiles_per_seq, x_ref, xp_ref, g_ref, h_s)
        zg_ref[...] = jnp.dot(h_s[CONV_HALO:CONV_HALO + tm, :], wg_ref[...], preferred_element_type=F32)
        grp_ref[...] = jnp.zeros_like(grp_ref)

    @pl.when(j < j_v)
    def _():
        u_s[...] = jnp.dot(h_s[...], w_ref[...], preferred_element_type=F32)
        y = _causal_conv(u_s, cw_ref, cb_ref, QK_CONV, tm)
        scale = jnp.where(j >= j_k, ATTN_SCALE, 1.0)
        o_ref[...] = (y * jax.nn.sigmoid(y) * scale).astype(o_ref.dtype)

    @pl.when((j >= j_v) & (j < j_nq))
    def _():
        o_ref[...] = jnp.dot(h_s[CONV_HALO:CONV_HALO + tm, :], w_ref[...],
                             preferred_element_type=F32).astype(o_ref.dtype)

    @pl.when(j >= j_nq)
    def _():
        is_q = j < j_nkv
        for r in range(tm // EPI_ROWS):
            rs = slice(r * EPI_ROWS, (r + 1) * EPI_ROWS)
            cosf, sinf = cos_ref[rs, :], sin_ref[rs, :]
            lane = lax.broadcasted_iota(jnp.int32, cosf.shape, 1)
            u = jnp.dot(h_s[CONV_HALO + r * EPI_ROWS:CONV_HALO + (r + 1) * EPI_ROWS, :], w_ref[...],
                        preferred_element_type=F32)
            for head in range(tn // HEAD_DIM):
                cs = slice(head * HEAD_DIM, (head + 1) * HEAD_DIM)
                xh = u[:, cs]
                roped = _rope(xh, cosf, sinf, lane)
                if head < NSA_KV_HEADS:
                    yh = roped * jnp.where(is_q, ATTN_SCALE, 1.0)
                else:
                    yh = jnp.where(is_q, roped * ATTN_SCALE, xh)
                o_ref[rs, cs] = yh.astype(o_ref.dtype)
                y_s[head, rs, :] = yh

        @pl.when(j == j_nkv)
        def _():
            ngrp = tm // CMP_STRIDE
            for head in range(tn // HEAD_DIM):
                for l in range(CMP_STRIDE):
                    col = (head * CMP_STRIDE + l) * HEAD_DIM
                    grp_ref[:, col:col + HEAD_DIM] = y_s[head, pl.ds(l, ngrp, stride=CMP_STRIDE),
                                                         :].astype(grp_ref.dtype)


def proj_in(x, g, w_z, w_gate, layer, conv_w, conv_b, cosf, sinf, seq):
    m, k = x.shape
    tm, tn, halo = MM_TM, MM_TN, CONV_HALO
    assert tn == 2 * NSA_KV_W and Z_MV % tn == 0 and Z_NQ % tn == 0 and Z_KC % tn == 0
    n_conv = Z_MV // tn
    nper = seq // tm
    conv_tile = lambda j: jnp.minimum(j, n_conv - 1)
    return pl.pallas_call(
        functools.partial(_proj_in_kernel, seq // tm),
        out_shape=(jax.ShapeDtypeStruct((m, Z_W), BF16), jax.ShapeDtypeStruct((m, LANES), F32),
                   jax.ShapeDtypeStruct((m // CMP_STRIDE, tn * CMP_STRIDE), BF16)),
        grid=(m // tm, Z_W // tn),
        in_specs=[
            pl.BlockSpec((tm, k), lambda i, j: (i, 0)),
            pl.BlockSpec((halo, k), lambda i, j: (jnp.maximum(i * (tm // halo) - 1, 0), 0)),
            pl.BlockSpec((1, k), lambda i, j: (0, 0)),
            pl.BlockSpec((None, k, tn), lambda i, j: (layer, 0, j)),
            pl.BlockSpec((None, k, LANES), lambda i, j: (layer, 0, 0)),
            pl.BlockSpec((QK_CONV, tn), lambda i, j: (0, conv_tile(j))),
            pl.BlockSpec((1, tn), lambda i, j: (0, conv_tile(j))),
            pl.BlockSpec((tm, LANES), lambda i, j: (i % nper, 0)),
            pl.BlockSpec((tm, LANES), lambda i, j: (i % nper, 0)),
        ],
        out_specs=(pl.BlockSpec((tm, tn), lambda i, j: (i, j)),
                   pl.BlockSpec((tm, LANES), lambda i, j: (i, 0)),
                   pl.BlockSpec((tm // CMP_STRIDE, tn * CMP_STRIDE), lambda i, j: (i, 0))),
        scratch_shapes=[pltpu.VMEM((tm + halo, k), BF16), pltpu.VMEM((tm + halo, tn), F32),
                        pltpu.VMEM((tn // HEAD_DIM, tm, HEAD_DIM), F32)],
        compiler_params=_cparams("parallel", "arbitrary"),
        name="proj_in",
    )(x, x, g.reshape(1, k), w_z, w_gate, conv_w, conv_b.reshape(1, -1), cosf, sinf)


def _matmul_res_kernel(n_lhs, *refs):
    lhs = refs[:n_lhs]
    ws = refs[n_lhs:2 * n_lhs]
    res_ref, o_ref = refs[2 * n_lhs], refs[2 * n_lhs + 1]
    acc = res_ref[...]
    for a_ref, w_ref in zip(lhs, ws):
        acc = acc + jnp.dot(a_ref[...], w_ref[...], preferred_element_type=F32)
    o_ref[...] = acc


def matmul_res(lhs_list, w, layer, res, tm, tn):
    m, n = res.shape
    n_lhs = len(lhs_list)
    kb = w.shape[1] // n_lhs
    assert all(a.shape[1] == kb for a in lhs_list)
    in_specs = [pl.BlockSpec((tm, kb), lambda i, j: (i, 0)) for _ in lhs_list]
    in_specs += [pl.BlockSpec((None, kb, tn), lambda i, j, p=p: (layer, p, j)) for p in range(n_lhs)]
    in_specs += [pl.BlockSpec((tm, tn), lambda i, j: (i, j))]
    return pl.pallas_call(
        functools.partial(_matmul_res_kernel, n_lhs),
        out_shape=jax.ShapeDtypeStruct((m, n), F32),
        grid=(m // tm, n // tn),
        in_specs=in_specs,
        out_specs=pl.BlockSpec((tm, tn), lambda i, j: (i, j)),
        compiler_params=_cparams("parallel", "arbitrary"),
        name="matmul_res",
    )(*lhs_list, *([w] * n_lhs), res)


def _mlstm_kernel(seq, q_ref, k_ref, v_ref, zo_ref, gi_ref, gf_ref, bi_ref, bf_ref, nrm_ref, o_ref,
                  b_s, i_s, cx_s, m_s):
    L = MLSTM_CHUNK
    hps = MLSTM_HEADS_PER_STEP
    nchunks = seq // L

    lane = lax.broadcasted_iota(jnp.int32, (nchunks, L), 1)
    for hh in range(hps):
        i_s[hh] = gi_ref[0, hh] + bi_ref[hh]
        fpre = gf_ref[0, hh] + bf_ref[hh]
        csum = jnp.minimum(fpre, 0.0) - jnp.log1p(jnp.exp(-jnp.abs(fpre)))
        for sh in (1 << e for e in range(L.bit_length() - 1)):
            csum = csum + jnp.where(lane >= sh, pltpu.roll(csum, sh, axis=1), 0.0)
        b_s[hh] = csum

    cx_s[...] = jnp.zeros_like(cx_s)
    m_s[...] = jnp.zeros_like(m_s)

    row = lax.broadcasted_iota(jnp.int32, (L, L), 0)
    col = lax.broadcasted_iota(jnp.int32, (L, L), 1)
    eye = row == col
    causal = col <= row
    ones = jnp.ones((L, LANES), BF16)

    def chunk_body(c, carry):
        r0 = pl.multiple_of(c * L, L)
        for hh in range(hps):
            cs = slice(hh * HEAD_DIM, (hh + 1) * HEAD_DIM)
            qc = q_ref[pl.ds(r0, L), cs]
            kc = k_ref[pl.ds(r0, L), cs]
            vx = jnp.concatenate([v_ref[pl.ds(r0, L), cs], ones], axis=1)
            b_row = b_s[hh, pl.ds(c, 1), :]
            u_row = i_s[hh, pl.ds(c, 1), :] - b_row
            b_col = jnp.sum(jnp.where(eye, b_row, 0.0), axis=-1, keepdims=True)
            m_prev = m_s[hh]
            cx = cx_s[hh]

            umat = jnp.where(causal, u_row, -jnp.inf)
            m_col = jnp.maximum(m_prev, jnp.max(umat, axis=-1, keepdims=True))
            w_inter = jnp.exp(m_prev - m_col)
            s_qk = _nt_dot(qc, kc) * jnp.exp(umat - m_col)
            nd = (w_inter * jnp.dot(qc, cx.astype(BF16), preferred_element_type=F32)
                  + jnp.dot(s_qk.astype(BF16), vx, preferred_element_type=F32))
            floor = jnp.exp(-(b_col + m_col))
            h = nd[:, :HEAD_DIM] / jnp.maximum(jnp.abs(nd[:, HEAD_DIM:]), floor)

            m_last = m_col[L - 1:L, :]
            kw_t = kc.astype(F32).T * jnp.exp(u_row - m_last)
            cx_s[hh] = (jnp.exp(m_prev - m_last) * cx
                        + jnp.dot(kw_t.astype(BF16), vx, preferred_element_type=F32))
            m_s[hh] = b_row[:, L - 1:L] + m_last

            hn = _rms(h, nrm_ref[:, cs])
            gate = jax.nn.sigmoid(zo_ref[pl.ds(r0, L), cs].astype(F32))
            o_ref[pl.ds(r0, L), cs] = (hn * gate).astype(o_ref.dtype)
        return carry

    lax.fori_loop(0, nchunks, chunk_body, 0)


def mlstm_group(z, gi, gf, bi, bf, norm_g, batch, seq):
    hps = MLSTM_HEADS_PER_STEP
    wblk = hps * HEAD_DIM
    nchunks = seq // MLSTM_CHUNK
    nhb = MLSTM_HEADS // hps
    zspec = lambda col0: pl.BlockSpec((seq, wblk), lambda b, h: (b, col0 // wblk + h))
    gspec = pl.BlockSpec((1, hps, nchunks, MLSTM_CHUNK), lambda b, h: (b, h, 0, 0))
    bspec = pl.BlockSpec((hps, 1, MLSTM_CHUNK), lambda b, h: (h, 0, 0))
    return pl.pallas_call(
        functools.partial(_mlstm_kernel, seq),
        out_shape=jax.ShapeDtypeStruct((batch * seq, MLSTM_W), BF16),
        grid=(batch, nhb),
        in_specs=[zspec(Z_MQ), zspec(Z_MK), zspec(Z_MV), zspec(Z_MO), gspec, gspec, bspec, bspec,
                  pl.BlockSpec((1, wblk), lambda b, h: (0, h))],
        out_specs=pl.BlockSpec((seq, wblk), lambda b, h: (b, h)),
        scratch_shapes=[
            pltpu.VMEM((hps, nchunks, MLSTM_CHUNK), F32),
            pltpu.VMEM((hps, nchunks, MLSTM_CHUNK), F32),
            pltpu.VMEM((hps, HEAD_DIM, 2 * HEAD_DIM), F32),
            pltpu.VMEM((hps, 1, 1), F32),
        ],
        compiler_params=_cparams("parallel", "parallel"),
        name="mlstm_group",
    )(z, z, z, z, gi, gf, bi, bf, norm_g.reshape(1, -1))


def _compress_kernel(x_ref, pe_ref, w1_ref, w2_ref, o_ref, sh_s):
    half = CMP_STRIDE * HEAD_DIM
    ngroups = x_ref.shape[0]
    x = x_ref[...].astype(F32)
    lo = jnp.dot((x + pe_ref[:, :half]).astype(BF16), w1_ref[:half, :], preferred_element_type=F32)
    hi = jnp.dot((x + pe_ref[:, half:]).astype(BF16), w1_ref[half:, :], preferred_element_type=F32)
    sh_s[0:ngroups, :] = hi
    sh_s[ngroups:ngroups + SUBLANES_F32, :] = jnp.zeros((SUBLANES_F32, CMP_HIDDEN), F32)
    pre = lo + sh_s[1:ngroups + 1, :]
    act = jax.nn.gelu(pre, approximate=True)
    o_ref[0] = jnp.dot(act.astype(BF16), w2_ref[...], preferred_element_type=F32).astype(o_ref.dtype)


def compress(grp, first_head, pe, w1, w2, layer, batch, seq):
    nslab, ngroups, gw = batch * NSA_KV_HEADS, seq // CMP_STRIDE, CMP_STRIDE * HEAD_DIM
    return pl.pallas_call(
        _compress_kernel,
        out_shape=jax.ShapeDtypeStruct((nslab, ngroups, HEAD_DIM), BF16),
        grid=(nslab,),
        in_specs=[pl.BlockSpec((ngroups, gw), lambda s: (s // NSA_KV_HEADS, first_head + s % NSA_KV_HEADS)),
                  pl.BlockSpec((1, 2 * gw), lambda s: (0, 0)),
                  pl.BlockSpec((None, 2 * gw, CMP_HIDDEN), lambda s: (layer, 0, 0)),
                  pl.BlockSpec((None, CMP_HIDDEN, HEAD_DIM), lambda s: (layer, 0, 0))],
        out_specs=pl.BlockSpec((1, ngroups, HEAD_DIM), lambda s: (s, 0, 0)),
        scratch_shapes=[pltpu.VMEM((ngroups + SUBLANES_F32, CMP_HIDDEN), F32)],
        compiler_params=_cparams("parallel"),
        name="nsa_compress",
    )(grp, pe.reshape(1, -1), w1, w2)


def _cmp_select_kernel(nsel, q_ref, kc_ref, vc_ref, ov_ref, o_ref, sel_ref):
    tq = q_ref.shape[0]
    s0 = pl.program_id(2) * tq
    kc = kc_ref[0]
    vc = vc_ref[0]
    ncp = kc.shape[0]
    pos = s0 + lax.broadcasted_iota(jnp.int32, (tq, ncp), 0)
    cidx = lax.broadcasted_iota(jnp.int32, (tq, ncp), 1)
    valid = cidx * CMP_STRIDE + (CMP_BLOCK - 1) <= pos
    any_valid = s0 + lax.broadcasted_iota(jnp.int32, (tq, 1), 0) >= CMP_BLOCK - 1
    sc = _nt_dot(_stack_heads(q_ref), kc)
    sc = jnp.where(valid[None], sc.reshape(NSA_GROUP, tq, ncp), NEG)
    e = jnp.where(valid[None], jnp.exp(sc - jnp.max(sc, axis=-1, keepdims=True)), 0.0)
    denom = jnp.sum(e, axis=-1, keepdims=True)
    p = e / jnp.where(any_valid[None], denom, 1.0)
    o = jnp.dot(p.reshape(NSA_GROUP * tq, ncp).astype(BF16), vc, preferred_element_type=F32)
    p_sum = p[0]
    for g in range(NSA_GROUP):
        o_ref[:, g * HEAD_DIM:(g + 1) * HEAD_DIM] = o[g * tq:(g + 1) * tq, :].astype(o_ref.dtype)
        if g:
            p_sum = p_sum + p[g]
    imp = jnp.dot(p_sum, ov_ref[...], preferred_element_type=F32, precision=lax.Precision.HIGHEST)
    jb = cidx
    cur = pos // SEL_BLOCK
    forced = (jb == 0) | (jb == cur) | (jb == cur - 1)
    imp = jnp.where(forced, FORCE_SCORE, imp)
    imp = jnp.where(jb <= cur, imp, -1.0)
    imp_t = imp.T[:nsel, :]
    jb_t = lax.broadcasted_iota(jnp.int32, (nsel, tq), 0)
    rank = jnp.zeros((nsel, tq), F32)
    for j2 in range(nsel):
        rowv = imp_t[j2:j2 + 1, :]
        before = (rowv > imp_t) | ((rowv == imp_t) & (jb_t > j2))
        rank = rank + jnp.where(before, 1.0, 0.0)
    sel_t = jnp.where(rank < float(SEL_TOPK), 1.0, 0.0)
    sel_t = jnp.concatenate([sel_t, jnp.zeros((ncp - nsel, tq), F32)], axis=0)
    sel_ref[...] = sel_t.T.astype(sel_ref.dtype)


def cmp_select(z, kcmp, vcmp, overlap, batch, seq):
    m = z.shape[0]
    tq = CMP_TQ
    nq = seq // tq
    ncp = kcmp.shape[1]
    gw = NSA_GROUP * HEAD_DIM
    return pl.pallas_call(
        functools.partial(_cmp_select_kernel, seq // SEL_BLOCK),
        out_shape=(jax.ShapeDtypeStruct((m, NSA_W), BF16),
                   jax.ShapeDtypeStruct((m, NSA_KV_HEADS * LANES), BF16)),
        grid=(batch, NSA_KV_HEADS, nq),
        in_specs=[pl.BlockSpec((tq, gw), lambda b, h, i: (b * nq + i, Z_NQ // gw + h)),
                  pl.BlockSpec((1, ncp, HEAD_DIM), lambda b, h, i: (b * NSA_KV_HEADS + h, 0, 0)),
                  pl.BlockSpec((1, ncp, HEAD_DIM), lambda b, h, i: (b * NSA_KV_HEADS + h, 0, 0)),
                  pl.BlockSpec((ncp, LANES), lambda b, h, i: (0, 0))],
        out_specs=(pl.BlockSpec((tq, gw), lambda b, h, i: (b * nq + i, h)),
                   pl.BlockSpec((tq, LANES), lambda b, h, i: (b * nq + i, h))),
        compiler_params=_cparams("parallel", "parallel", "parallel"),
        name="nsa_cmp_select",
    )(z, kcmp, vcmp, overlap)


def _sel_attn_kernel(q_ref, k_ref, v_ref, oh_ref, sel_ref, o_ref, qx_s, m_s, acc_s, s0_s, s1_s):
    tq, tk = ATTN_TQ, ATTN_TK
    i = pl.program_id(2)
    penalty = ((sel_ref[...].astype(F32) - 1.0) * (-NEG)).astype(BF16)
    for g in range(NSA_GROUP):
        cs = slice(g * HEAD_DIM, (g + 1) * HEAD_DIM)
        qx_s[g * tq:(g + 1) * tq, :] = jnp.concatenate([q_ref[:, cs], penalty], axis=1)
    m_s[...] = jnp.full_like(m_s, NEG)
    acc_s[...] = jnp.zeros_like(acc_s)
    ones = jnp.ones((tk, LANES), BF16)

    def scores(t, s_buf):
        k0 = pl.multiple_of(t * tk, tk)
        kx = jnp.concatenate([k_ref[pl.ds(k0, tk), :], oh_ref[pl.ds(k0, tk), :]], axis=1)
        s_buf[...] = _nt_dot(qx_s[...], kx)

    def update(t, s_buf, diagonal):
        k0 = pl.multiple_of(t * tk, tk)
        vx = jnp.concatenate([v_ref[pl.ds(k0, tk), :], ones], axis=1)
        s = s_buf[...]
        if diagonal:
            row = lax.broadcasted_iota(jnp.int32, (tq, tk), 0)
            col = lax.broadcasted_iota(jnp.int32, (tq, tk), 1)
            causal = jnp.where(col <= row, 0.0, NEG)
            s = (s.reshape(NSA_GROUP, tq, tk) + causal[None]).reshape(NSA_GROUP * tq, tk)
        m_old = m_s[...]
        m_new = jnp.maximum(m_old, jnp.max(s, axis=-1, keepdims=True))
        alpha = jnp.exp(m_old - m_new)
        p = jnp.exp((s - jnp.concatenate([m_new] * (tk // LANES), axis=1)).astype(BF16))
        acc_s[...] = (jnp.concatenate([alpha, alpha], axis=1) * acc_s[...]
                      + jnp.dot(p, vx, preferred_element_type=F32))
        m_s[...] = m_new

    scores(0, s0_s)

    def pair(p, carry):
        t = 2 * p
        scores(t + 1, s1_s)
        update(t, s0_s, False)
        scores(t + 2, s0_s)
        update(t + 1, s1_s, False)
        return carry

    lax.fori_loop(0, i // 2, pair, 0)

    @pl.when(i % 2 == 0)
    def _():
        update(i, s0_s, True)

    @pl.when(i % 2 == 1)
    def _():
        scores(i, s1_s)
        update(i - 1, s0_s, False)
        update(i, s1_s, True)
    for g in range(NSA_GROUP):
        cs = slice(g * HEAD_DIM, (g + 1) * HEAD_DIM)
        acc = acc_s[g * tq:(g + 1) * tq, :]
        o_ref[:, cs] = (acc[:, :HEAD_DIM] / acc[:, HEAD_DIM:]).astype(o_ref.dtype)


def _kv_spec(seq, col0):
    return pl.BlockSpec((seq, HEAD_DIM), lambda b, h, i: (b, col0 // HEAD_DIM + h))


def sel_attn(z, onehot, sel, batch, seq):
    assert ATTN_TQ == ATTN_TK
    m = z.shape[0]
    tq = ATTN_TQ
    nq = seq // tq
    gw = NSA_GROUP * HEAD_DIM
    return pl.pallas_call(
        _sel_attn_kernel,
        out_shape=jax.ShapeDtypeStruct((m, NSA_W), BF16),
        grid=(batch, NSA_KV_HEADS, nq),
        in_specs=[pl.BlockSpec((tq, gw), lambda b, h, i: (b * nq + i, Z_NQ // gw + h)),
                  _kv_spec(seq, Z_KS), _kv_spec(seq, Z_VS),
                  pl.BlockSpec((seq, LANES), lambda b, h, i: (0, 0)),
                  pl.BlockSpec((tq, LANES), lambda b, h, i: (b * nq + i, h))],
        out_specs=pl.BlockSpec((tq, gw), lambda b, h, i: (b * nq + i, h)),
        scratch_shapes=[pltpu.VMEM((NSA_GROUP * tq, 2 * LANES), BF16),
                        pltpu.VMEM((NSA_GROUP * tq, LANES), F32),
                        pltpu.VMEM((NSA_GROUP * tq, 2 * LANES), F32),
                        pltpu.VMEM((NSA_GROUP * tq, ATTN_TK), F32),
                        pltpu.VMEM((NSA_GROUP * tq, ATTN_TK), F32)],
        compiler_params=_cparams("parallel", "parallel", "arbitrary"),
        name="nsa_sel_attn",
    )(z, z, z, onehot, sel)


def _win_attn_kernel(q_ref, k_ref, v_ref, o_ref):
    tq, nk = ATTN_TQ, WIN_KEYS
    s0 = pl.program_id(2) * tq
    start = pl.multiple_of(jnp.maximum(s0 - WINDOW, 0), tq)
    kw = k_ref[pl.ds(start, nk), :]
    vx = jnp.concatenate([v_ref[pl.ds(start, nk), :], jnp.ones((nk, LANES), BF16)], axis=1)
    qpos = s0 + lax.broadcasted_iota(jnp.int32, (tq, nk), 0)
    kpos = start + lax.broadcasted_iota(jnp.int32, (tq, nk), 1)
    diff = qpos - kpos
    bias = jnp.where((diff >= 0) & (diff < WINDOW), 0.0, NEG)
    for g in range(NSA_GROUP):
        cs = slice(g * HEAD_DIM, (g + 1) * HEAD_DIM)
        s = _nt_dot(q_ref[:, cs], kw) + bias
        p = jnp.exp((s - jnp.max(s, axis=-1, keepdims=True)).astype(BF16))
        a = jnp.dot(p, vx, preferred_element_type=F32)
        o_ref[:, cs] = (a[:, :HEAD_DIM] / a[:, HEAD_DIM:]).astype(o_ref.dtype)


def win_attn(z, batch, seq):
    m = z.shape[0]
    tq = ATTN_TQ
    nq = seq // tq
    gw = NSA_GROUP * HEAD_DIM
    return pl.pallas_call(
        _win_attn_kernel,
        out_shape=jax.ShapeDtypeStruct((m, NSA_W), BF16),
        grid=(batch, NSA_KV_HEADS, nq),
        in_specs=[pl.BlockSpec((tq, gw), lambda b, h, i: (b * nq + i, Z_NQ // gw + h)),
                  _kv_spec(seq, Z_KW), _kv_spec(seq, Z_VW)],
        out_specs=pl.BlockSpec((tq, gw), lambda b, h, i: (b * nq + i, h)),
        compiler_params=_cparams("parallel", "parallel", "parallel"),
        name="nsa_win_attn",
    )(z, z, z)


def _nsa_combine_kernel(oc_ref, os_ref, ow_ref, zg_ref, ex_ref, nrm_ref, o_ref, mix_s):
    tm = oc_ref.shape[0]
    gates = jax.nn.sigmoid(zg_ref[...])
    g_hi = gates.astype(BF16)
    g_lo = (gates - g_hi.astype(F32)).astype(BF16)
    mix = None
    for br, src in enumerate((oc_ref, os_ref, ow_ref)):
        spread = (jnp.dot(g_hi, ex_ref[br], preferred_element_type=F32)
                  + jnp.dot(g_lo, ex_ref[br], preferred_element_type=F32))
        term = spread * src[...].astype(F32)
        mix = term if mix is None else mix + term
    mix_s[...] = mix
    inv = lax.rsqrt(jnp.mean(mix * mix, axis=-1, keepdims=True) + NORM_EPS)
    o_ref[...] = (mix_s[...] * inv * nrm_ref[...]).astype(o_ref.dtype)


def _gate_spread():
    col = jnp.arange(LANES)[None, :, None]
    head = (jnp.arange(NSA_W) // HEAD_DIM)[None, None, :]
    br = jnp.arange(3)[:, None, None]
    return (col == ZG_NG + 3 * head + br).astype(BF16)


def nsa_combine(o_cmp, o_slc, o_win, zg, norm_g):
    m = o_cmp.shape[0]
    tm = ROW_TM
    ospec = pl.BlockSpec((tm, NSA_W), lambda i: (i, 0))
    return pl.pallas_call(
        _nsa_combine_kernel,
        out_shape=jax.ShapeDtypeStruct((m, NSA_W), BF16),
        grid=(m // tm,),
        in_specs=[ospec, ospec, ospec, pl.BlockSpec((tm, LANES), lambda i: (i, 0)),
                  pl.BlockSpec((3, LANES, NSA_W), lambda i: (0, 0, 0)),
                  pl.BlockSpec((1, NSA_W), lambda i: (0, 0))],
        out_specs=ospec,
        scratch_shapes=[pltpu.VMEM((tm, NSA_W), F32)],
        compiler_params=_cparams("parallel"),
        name="nsa_combine",
    )(o_cmp, o_slc, o_win, zg, _gate_spread(), norm_g.reshape(1, -1))


def _ffn_up_kernel(tiles_per_seq, x_ref, xp_ref, g_ref, wg_ref, wu_ref, cwg_ref, cwu_ref,
                   cbg_ref, cbu_ref, o_ref, h_s, ug_s, uu_s):
    tm = x_ref.shape[0]

    @pl.when(pl.program_id(1) == 0)
    def _():
        _fill_halo_h(tiles_per_seq, x_ref, xp_ref, g_ref, h_s)

    ug_s[...] = jnp.dot(h_s[...], wg_ref[...], preferred_element_type=F32)
    uu_s[...] = jnp.dot(h_s[...], wu_ref[...], preferred_element_type=F32)
    gate = _causal_conv(ug_s, cwg_ref, cbg_ref, FFN_CONV, tm)
    up = _causal_conv(uu_s, cwu_ref, cbu_ref, FFN_CONV, tm)
    o_ref[...] = (gate * jax.nn.sigmoid(gate) * up).astype(o_ref.dtype)


def ffn_up(x, g, w_up, layer, conv_w, conv_b, seq):
    m, k = x.shape
    tm, tn, halo = MM_TM, MM_TN, CONV_HALO
    ntile = D_FF // tn
    cb = conv_b.reshape(1, -1)
    return pl.pallas_call(
        functools.partial(_ffn_up_kernel, seq // tm),
        out_shape=jax.ShapeDtypeStruct((m, D_FF), BF16),
        grid=(m // tm, ntile),
        in_specs=[
            pl.BlockSpec((tm, k), lambda i, j: (i, 0)),
            pl.BlockSpec((halo, k), lambda i, j: (jnp.maximum(i * (tm // halo) - 1, 0), 0)),
            pl.BlockSpec((1, k), lambda i, j: (0, 0)),
            pl.BlockSpec((None, k, tn), lambda i, j: (layer, 0, j)),
            pl.BlockSpec((None, k, tn), lambda i, j: (layer, 0, ntile + j)),
            pl.BlockSpec((FFN_CONV, tn), lambda i, j: (0, j)),
            pl.BlockSpec((FFN_CONV, tn), lambda i, j: (0, ntile + j)),
            pl.BlockSpec((1, tn), lambda i, j: (0, j)),
            pl.BlockSpec((1, tn), lambda i, j: (0, ntile + j)),
        ],
        out_specs=pl.BlockSpec((tm, tn), lambda i, j: (i, j)),
        scratch_shapes=[pltpu.VMEM((tm + halo, k), BF16),
                        pltpu.VMEM((tm + halo, tn), F32),
                        pltpu.VMEM((tm + halo, tn), F32)],
        compiler_params=_cparams("parallel", "arbitrary"),
        name="ffn_up",
    )(x, x, g.reshape(1, k), w_up, w_up, conv_w, conv_w, cb, cb)


def _rmsnorm_kernel(x_ref, g_ref, o_ref):
    o_ref[...] = _rms(x_ref[...], g_ref[...])


def rmsnorm(x, g):
    m, k = x.shape
    tm = ROW_TM
    return pl.pallas_call(
        _rmsnorm_kernel,
        out_shape=jax.ShapeDtypeStruct((m, k), F32),
        grid=(m // tm,),
        in_specs=[pl.BlockSpec((tm, k), lambda i: (i, 0)), pl.BlockSpec((1, k), lambda i: (0, 0))],
        out_specs=pl.BlockSpec((tm, k), lambda i: (i, 0)),
        compiler_params=_cparams("parallel"),
        name="final_rmsnorm",
    )(x, g.reshape(1, k))


def _gate_weight(w_in):
    gates = jnp.concatenate([w_in[..., W_IN_MI:W_IN_MI + 2 * MLSTM_HEADS],
                             w_in[..., W_IN_NG:W_IN_NG + 3 * NSA_HEADS]], axis=-1)
    return jnp.pad(gates, ((0, 0), (0, 0), (0, LANES - gates.shape[-1]))).astype(BF16)


def _rope_tables(seq):
    pos = jnp.arange(seq, dtype=F32)
    inv = ROPE_THETA ** (-jnp.arange(0, ROPE_DIM, 2, dtype=F32) / ROPE_DIM)
    ang = pos[:, None] * inv[None, :]
    cos, sin = jnp.cos(ang), jnp.sin(ang)
    rest = HEAD_DIM - ROPE_DIM
    cosf = jnp.concatenate([cos, cos, jnp.ones((seq, rest), F32)], axis=1)
    sinf = jnp.concatenate([-sin, sin, jnp.zeros((seq, rest), F32)], axis=1)
    return cosf, sinf


def _overlap_matrix(seq, ncmp_pad):
    nsel = seq // SEL_BLOCK
    cstart = jnp.arange(ncmp_pad) * CMP_STRIDE
    jb = jnp.arange(LANES)
    ov = ((cstart[:, None] < (jb[None, :] + 1) * SEL_BLOCK)
          & (cstart[:, None] + CMP_BLOCK > jb[None, :] * SEL_BLOCK) & (jb[None, :] < nsel))
    return ov.astype(F32)


def _block_onehot(seq):
    blk = jnp.arange(seq)[:, None] // SEL_BLOCK
    return (blk == jnp.arange(LANES)[None, :]).astype(BF16)


def _layer(x, p, wts, layer, cosf, sinf, overlap, onehot, batch, seq):
    z, zg, grp = proj_in(x, p["attn_norm"], wts["w_z"], wts["w_gate"], layer, p["qk_conv_w"],
                         p["qk_conv_b"], cosf, sinf, seq)

    nchunks = seq // MLSTM_CHUNK
    gate_rows = lambda c0: (zg[:, c0:c0 + MLSTM_HEADS].reshape(batch, seq, MLSTM_HEADS)
                            .transpose(0, 2, 1).reshape(batch, MLSTM_HEADS, nchunks, MLSTM_CHUNK))
    lanes = lambda v: jnp.broadcast_to(v.reshape(MLSTM_HEADS, 1, 1), (MLSTM_HEADS, 1, MLSTM_CHUNK))
    mix_a = mlstm_group(z, gate_rows(ZG_MI), gate_rows(ZG_MF), lanes(p["i_bias"]),
                        lanes(p["f_bias"]), p["mlstm_norm"], batch, seq)

    kcmp = compress(grp, 0, p["cmp_pe_k"], wts["cmp_w1_k"], wts["cmp_w2_k"], layer, batch, seq)
    vcmp = compress(grp, NSA_KV_HEADS, p["cmp_pe_v"], wts["cmp_w1_v"], wts["cmp_w2_v"], layer, batch, seq)
    o_cmp, sel = cmp_select(z, kcmp, vcmp, overlap, batch, seq)
    o_slc = sel_attn(z, onehot, sel, batch, seq)
    o_win = win_attn(z, batch, seq)
    mix_b = nsa_combine(o_cmp, o_slc, o_win, zg, p["nsa_norm"])

    x = matmul_res([mix_a, mix_b], wts["w_out"], layer, x, OUT_TM, D_MODEL)

    act = ffn_up(x, p["ffn_norm"], wts["w_up"], layer, p["ffn_conv_w"], p["ffn_conv_b"], seq)
    x = matmul_res([act], wts["w_down"], layer, x, MM_TM, DOWN_TN)
    return x


_LAYER_PARAMS = ("attn_norm", "w_in", "qk_conv_w", "qk_conv_b", "i_bias", "f_bias", "mlstm_norm",
                 "cmp_pe_k", "cmp_pe_v", "cmp_w1_k", "cmp_w2_k", "cmp_w1_v", "cmp_w2_v", "nsa_norm",
                 "w_out", "ffn_norm", "w_up", "ffn_conv_w", "ffn_conv_b", "w_down")


def kernel(x, attn_norm, w_in, qk_conv_w, qk_conv_b, i_bias, f_bias, mlstm_norm, cmp_pe_k, cmp_pe_v,
           cmp_w1_k, cmp_w2_k, cmp_w1_v, cmp_w2_v, nsa_norm, w_out, ffn_norm, w_up, ffn_conv_w,
           ffn_conv_b, w_down, final_norm):
    stacked = dict(zip(_LAYER_PARAMS, (attn_norm, w_in, qk_conv_w, qk_conv_b, i_bias, f_bias,
                                       mlstm_norm, cmp_pe_k, cmp_pe_v, cmp_w1_k, cmp_w2_k, cmp_w1_v,
                                       cmp_w2_v, nsa_norm, w_out, ffn_norm, w_up, ffn_conv_w,
                                       ffn_conv_b, w_down)))
    batch, seq, d = x.shape
    assert d == D_MODEL and seq % MM_TM == 0 and seq % ATTN_TQ == 0 and seq >= WIN_KEYS
    assert seq // CMP_STRIDE == LANES and seq // SEL_BLOCK <= LANES
    cosf, sinf = _rope_tables(seq)
    overlap = _overlap_matrix(seq, seq // CMP_STRIDE)
    onehot = _block_onehot(seq)
    big = ("w_out", "w_up", "w_down", "cmp_w1_k", "cmp_w2_k", "cmp_w1_v", "cmp_w2_v")
    wts = {k: stacked[k].astype(BF16) for k in big}
    wts["w_z"] = jnp.concatenate([w_in[..., :W_IN_MI], w_in[..., W_IN_NSA[0]:W_IN_NSA[1]]],
                                 axis=-1).astype(BF16)
    wts["w_gate"] = _gate_weight(w_in)
    xf = x.reshape(batch * seq, d)
    for layer in range(DEPTH):
        p = {k: v[layer] for k, v in stacked.items() if k not in big and k != "w_in"}
        xf = _layer(xf, p, wts, layer, cosf, sinf, overlap, onehot, batch, seq)
    return rmsnorm(xf, final_norm).reshape(batch, seq, d)
```

```python
import functools

import jax
import jax.numpy as jnp
from jax import lax
from jax.experimental import pallas as pl
from jax.experimental.pallas import tpu as pltpu

D_MODEL = 2048
DEPTH = 2
MLSTM_HEADS = 8
HEAD_DIM = 128
MLSTM_W = MLSTM_HEADS * HEAD_DIM
QK_CONV = 4
NSA_HEADS = 8
NSA_KV_HEADS = 2
NSA_GROUP = NSA_HEADS // NSA_KV_HEADS
NSA_W = NSA_HEADS * HEAD_DIM
NSA_KV_W = NSA_KV_HEADS * HEAD_DIM
CMP_BLOCK = 32
CMP_STRIDE = 16
CMP_HIDDEN = 256
SEL_BLOCK = 64
SEL_TOPK = 16
WINDOW = 512
ROPE_THETA = 500000.0
ROPE_DIM = HEAD_DIM // 4
D_FF = 5632
FFN_CONV = 3
NORM_EPS = 1e-6
NEG = -1e30
FORCE_SCORE = 1e4
ATTN_SCALE = HEAD_DIM ** -0.5

LANES = 128
SUBLANES_F32 = 8
SUBLANES_BF16 = 16
VMEM_LIMIT_BYTES = 48 * 1024 * 1024

MM_TM = 1024
MM_TN = 512
EPI_ROWS = 128
DOWN_TN = 512
OUT_TM = 512
MLSTM_CHUNK = 256
MLSTM_HEADS_PER_STEP = 4
ATTN_TQ = 256
CMP_TQ = 512
ATTN_TK = 256
WIN_KEYS = WINDOW + ATTN_TQ
ROW_TM = 512
CONV_HALO = SUBLANES_BF16

BF16 = jnp.bfloat16
F32 = jnp.float32

W_IN_MI = 4 * MLSTM_W
W_IN_NSA = (W_IN_MI + 2 * MLSTM_HEADS, W_IN_MI + 2 * MLSTM_HEADS + NSA_W + 6 * NSA_KV_W)
W_IN_NG = W_IN_NSA[1]
Z_MQ, Z_MK, Z_MV, Z_MO, Z_NQ = (i * MLSTM_W for i in range(5))
Z_KC, Z_VC, Z_KS, Z_VS, Z_KW, Z_VW = (Z_NQ + NSA_W + i * NSA_KV_W for i in range(6))
Z_W = Z_VW + NSA_KV_W
ZG_MI, ZG_MF, ZG_NG = 0, MLSTM_HEADS, 2 * MLSTM_HEADS


def _cparams(*sem):
    return pltpu.CompilerParams(dimension_semantics=sem, vmem_limit_bytes=VMEM_LIMIT_BYTES)


def _rms(x, gain):
    return x * lax.rsqrt(jnp.mean(x * x, axis=-1, keepdims=True) + NORM_EPS) * gain


def _nt_dot(a, b):
    return lax.dot_general(a, b, (((1,), (1,)), ((), ())), preferred_element_type=F32)


def _stack_heads(q_ref):
    return jnp.concatenate([q_ref[:, g * HEAD_DIM:(g + 1) * HEAD_DIM] for g in range(NSA_GROUP)], axis=0)


def _fill_halo_h(tiles_per_seq, x_ref, xp_ref, g_ref, h_s):
    tm = x_ref.shape[0]
    first = pl.program_id(0) % tiles_per_seq == 0
    hp = _rms(xp_ref[...], g_ref[...])
    h_s[0:CONV_HALO, :] = jnp.where(first, 0.0, hp).astype(h_s.dtype)
    h_s[CONV_HALO:CONV_HALO + tm, :] = _rms(x_ref[...], g_ref[...]).astype(h_s.dtype)


def _causal_conv(u_s, cw_ref, cb_ref, taps, tm):
    u = u_s[...]
    acc = jnp.broadcast_to(cb_ref[...], (tm, u_s.shape[1]))
    for j in range(taps):
        back = taps - 1 - j
        shifted = pltpu.roll(u, back, axis=0) if back else u
        acc = acc + shifted[CONV_HALO:CONV_HALO + tm, :] * cw_ref[j:j + 1, :]
    return acc


def _rope(x, cosf, sinf, lane):
    rot = jnp.where(lane < ROPE_DIM // 2, pltpu.roll(x, LANES - ROPE_DIM // 2, axis=1),
                    pltpu.roll(x, ROPE_DIM // 2, axis=1))
    return x * cosf + rot * sinf


def _proj_in_kernel(tiles_per_seq, x_ref, xp_ref, g_ref, w_ref, wn_ref, wg_ref, cw_ref, cb_ref, cos_ref, sin_ref,
                    o_ref, zg_ref, grp_ref, h_s, u_s, y_s):
    tm, tn = o_ref.shape
    j = pl.program_id(1)
    j_k, j_v, j_nq, j_nkv = Z_MK // tn, Z_MV // tn, Z_NQ // tn, Z_KC // tn

    @pl.when(j == 0)
    def _():
        _fill_halo_h(tiles_per_seq, x_ref, xp_ref, g_ref, h_s)
        zg_ref[...] = jnp.dot(h_s[CONV_HALO:CONV_HALO + tm, :], wg_ref[...], preferred_element_type=F32)
        grp_ref[...] = jnp.zeros_like(grp_ref)

    @pl.when(j < j_v)
    def _():
        u_s[...] = jnp.dot(h_s[...], w_ref[...], preferred_element_type=F32)
        y = _causal_conv(u_s, cw_ref, cb_ref, QK_CONV, tm)
        scale = jnp.where(j >= j_k, ATTN_SCALE, 1.0)
        o_ref[...] = (y * jax.nn.sigmoid(y) * scale).astype(o_ref.dtype)

    @pl.when((j >= j_v) & (j < j_nq))
    def _():
        o_ref[...] = jnp.dot(h_s[CONV_HALO:CONV_HALO + tm, :], w_ref[...],
                             preferred_element_type=F32).astype(o_ref.dtype)

    @pl.when(j >= j_nq)
    def _():
        is_q = j < j_nkv
        for r in range(tm // EPI_ROWS):
            rs = slice(r * EPI_ROWS, (r + 1) * EPI_ROWS)
            cosf, sinf = cos_ref[rs, :], sin_ref[rs, :]
            lane = lax.broadcasted_iota(jnp.int32, cosf.shape, 1)
            u = jnp.dot(h_s[CONV_HALO + r * EPI_ROWS:CONV_HALO + (r + 1) * EPI_ROWS, :], wn_ref[...],
                        preferred_element_type=F32)
            for head in range(tn // HEAD_DIM):
                cs = slice(head * HEAD_DIM, (head + 1) * HEAD_DIM)
                xh = u[:, cs]
                roped = _rope(xh, cosf, sinf, lane)
                if head < NSA_KV_HEADS:
                    yh = roped * jnp.where(is_q, ATTN_SCALE, 1.0)
                else:
                    yh = jnp.where(is_q, roped * ATTN_SCALE, xh)
                o_ref[rs, cs] = yh.astype(o_ref.dtype)
                y_s[head, rs, :] = yh

        @pl.when(j == j_nkv)
        def _():
            ngrp = tm // CMP_STRIDE
            for head in range(tn // HEAD_DIM):
                for l in range(CMP_STRIDE):
                    col = (head * CMP_STRIDE + l) * HEAD_DIM
                    grp_ref[:, col:col + HEAD_DIM] = y_s[head, pl.ds(l, ngrp, stride=CMP_STRIDE),
                                                         :].astype(grp_ref.dtype)


def proj_in(x, g, w_in, w_nsa, w_gate, layer, conv_w, conv_b, cosf, sinf, seq):
    m, k = x.shape
    tm, tn, halo = MM_TM, MM_TN, CONV_HALO
    assert tn == 2 * NSA_KV_W and Z_MV % tn == 0 and Z_NQ % tn == 0 and Z_KC % tn == 0
    n_conv = Z_MV // tn
    n_m = Z_NQ // tn
    nper = seq // tm
    conv_tile = lambda j: jnp.minimum(j, n_conv - 1)
    return pl.pallas_call(
        functools.partial(_proj_in_kernel, seq // tm),
        out_shape=(jax.ShapeDtypeStruct((m, Z_W), BF16), jax.ShapeDtypeStruct((m, LANES), F32),
                   jax.ShapeDtypeStruct((m // CMP_STRIDE, tn * CMP_STRIDE), BF16)),
        grid=(m // tm, Z_W // tn),
        in_specs=[
            pl.BlockSpec((tm, k), lambda i, j: (i, 0)),
            pl.BlockSpec((halo, k), lambda i, j: (jnp.maximum(i * (tm // halo) - 1, 0), 0)),
            pl.BlockSpec((1, k), lambda i, j: (0, 0)),
            pl.BlockSpec((None, k, tn), lambda i, j: (layer, 0, jnp.minimum(j, n_m - 1))),
            pl.BlockSpec((None, k, tn), lambda i, j: (layer, 0, jnp.maximum(j - n_m, 0))),
            pl.BlockSpec((None, k, LANES), lambda i, j: (layer, 0, 0)),
            pl.BlockSpec((QK_CONV, tn), lambda i, j: (0, conv_tile(j))),
            pl.BlockSpec((1, tn), lambda i, j: (0, conv_tile(j))),
            pl.BlockSpec((tm, LANES), lambda i, j: (i % nper, 0)),
            pl.BlockSpec((tm, LANES), lambda i, j: (i % nper, 0)),
        ],
        out_specs=(pl.BlockSpec((tm, tn), lambda i, j: (i, j)),
                   pl.BlockSpec((tm, LANES), lambda i, j: (i, 0)),
                   pl.BlockSpec((tm // CMP_STRIDE, tn * CMP_STRIDE), lambda i, j: (i, 0))),
        scratch_shapes=[pltpu.VMEM((tm + halo, k), BF16), pltpu.VMEM((tm + halo, tn), F32),
                        pltpu.VMEM((tn // HEAD_DIM, tm, HEAD_DIM), F32)],
        compiler_params=_cparams("parallel", "arbitrary"),
        name="proj_in",
    )(x, x, g.reshape(1, k), w_in, w_nsa, w_gate, conv_w, conv_b.reshape(1, -1), cosf, sinf)


def _matmul_res_kernel(n_lhs, *refs):
    lhs = refs[:n_lhs]
    ws = refs[n_lhs:2 * n_lhs]
    res_ref, o_ref = refs[2 * n_lhs], refs[2 * n_lhs + 1]
    acc = res_ref[...]
    for a_ref, w_ref in zip(lhs, ws):
        acc = acc + jnp.dot(a_ref[...], w_ref[...], preferred_element_type=F32)
    o_ref[...] = acc


def matmul_res(lhs_list, w, layer, res, tm, tn):
    m, n = res.shape
    n_lhs = len(lhs_list)
    kb = w.shape[1] // n_lhs
    assert all(a.shape[1] == kb for a in lhs_list)
    in_specs = [pl.BlockSpec((tm, kb), lambda i, j: (i, 0)) for _ in lhs_list]
    in_specs += [pl.BlockSpec((None, kb, tn), lambda i, j, p=p: (layer, p, j)) for p in range(n_lhs)]
    in_specs += [pl.BlockSpec((tm, tn), lambda i, j: (i, j))]
    return pl.pallas_call(
        functools.partial(_matmul_res_kernel, n_lhs),
        out_shape=jax.ShapeDtypeStruct((m, n), F32),
        grid=(m // tm, n // tn),
        in_specs=in_specs,
        out_specs=pl.BlockSpec((tm, tn), lambda i, j: (i, j)),
        compiler_params=_cparams("parallel", "arbitrary"),
        name="matmul_res",
    )(*lhs_list, *([w] * n_lhs), res)


def _mlstm_kernel(seq, q_ref, k_ref, v_ref, zo_ref, gi_ref, gf_ref, bi_ref, bf_ref, nrm_ref, o_ref,
                  b_s, i_s, cx_s, m_s):
    L = MLSTM_CHUNK
    hps = MLSTM_HEADS_PER_STEP
    nchunks = seq // L

    lane = lax.broadcasted_iota(jnp.int32, (nchunks, L), 1)
    for hh in range(hps):
        i_s[hh] = gi_ref[0, hh] + bi_ref[hh]
        fpre = gf_ref[0, hh] + bf_ref[hh]
        csum = jnp.minimum(fpre, 0.0) - jnp.log1p(jnp.exp(-jnp.abs(fpre)))
        for sh in (1 << e for e in range(L.bit_length() - 1)):
            csum = csum + jnp.where(lane >= sh, pltpu.roll(csum, sh, axis=1), 0.0)
        b_s[hh] = csum

    cx_s[...] = jnp.zeros_like(cx_s)
    m_s[...] = jnp.zeros_like(m_s)

    row = lax.broadcasted_iota(jnp.int32, (L, L), 0)
    col = lax.broadcasted_iota(jnp.int32, (L, L), 1)
    eye = row == col
    causal = col <= row
    ones = jnp.ones((L, LANES), BF16)

    def chunk_body(c, carry):
        r0 = pl.multiple_of(c * L, L)
        for hh in range(hps):
            cs = slice(hh * HEAD_DIM, (hh + 1) * HEAD_DIM)
            qc = q_ref[pl.ds(r0, L), cs]
            kc = k_ref[pl.ds(r0, L), cs]
            vx = jnp.concatenate([v_ref[pl.ds(r0, L), cs], ones], axis=1)
            b_row = b_s[hh, pl.ds(c, 1), :]
            u_row = i_s[hh, pl.ds(c, 1), :] - b_row
            b_col = jnp.sum(jnp.where(eye, b_row, 0.0), axis=-1, keepdims=True)
            m_prev = m_s[hh]
            cx = cx_s[hh]

            umat = jnp.where(causal, u_row, -jnp.inf)
            m_col = jnp.maximum(m_prev, jnp.max(umat, axis=-1, keepdims=True))
            w_inter = jnp.exp(m_prev - m_col)
            s_qk = _nt_dot(qc, kc) * jnp.exp(umat - m_col)
            nd = (w_inter * jnp.dot(qc, cx.astype(BF16), preferred_element_type=F32)
                  + jnp.dot(s_qk.astype(BF16), vx, preferred_element_type=F32))
            floor = jnp.exp(-(b_col + m_col))
            h = nd[:, :HEAD_DIM] / jnp.maximum(jnp.abs(nd[:, HEAD_DIM:]), floor)

            m_last = m_col[L - 1:L, :]
            kw_t = kc.astype(F32).T * jnp.exp(u_row - m_last)
            cx_s[hh] = (jnp.exp(m_prev - m_last) * cx
                        + jnp.dot(kw_t.astype(BF16), vx, preferred_element_type=F32))
            m_s[hh] = b_row[:, L - 1:L] + m_last

            hn = _rms(h, nrm_ref[:, cs])
            gate = jax.nn.sigmoid(zo_ref[pl.ds(r0, L), cs].astype(F32))
            o_ref[pl.ds(r0, L), cs] = (hn * gate).astype(o_ref.dtype)
        return carry

    lax.fori_loop(0, nchunks, chunk_body, 0)


def mlstm_group(z, gi, gf, bi, bf, norm_g, batch, seq):
    hps = MLSTM_HEADS_PER_STEP
    wblk = hps * HEAD_DIM
    nchunks = seq // MLSTM_CHUNK
    nhb = MLSTM_HEADS // hps
    zspec = lambda col0: pl.BlockSpec((seq, wblk), lambda b, h: (b, col0 // wblk + h))
    gspec = pl.BlockSpec((1, hps, nchunks, MLSTM_CHUNK), lambda b, h: (b, h, 0, 0))
    bspec = pl.BlockSpec((hps, 1, MLSTM_CHUNK), lambda b, h: (h, 0, 0))
    return pl.pallas_call(
        functools.partial(_mlstm_kernel, seq),
        out_shape=jax.ShapeDtypeStruct((batch * seq, MLSTM_W), BF16),
        grid=(batch, nhb),
        in_specs=[zspec(Z_MQ), zspec(Z_MK), zspec(Z_MV), zspec(Z_MO), gspec, gspec, bspec, bspec,
                  pl.BlockSpec((1, wblk), lambda b, h: (0, h))],
        out_specs=pl.BlockSpec((seq, wblk), lambda b, h: (b, h)),
        scratch_shapes=[
            pltpu.VMEM((hps, nchunks, MLSTM_CHUNK), F32),
            pltpu.VMEM((hps, nchunks, MLSTM_CHUNK), F32),
            pltpu.VMEM((hps, HEAD_DIM, 2 * HEAD_DIM), F32),
            pltpu.VMEM((hps, 1, 1), F32),
        ],
        compiler_params=_cparams("parallel", "parallel"),
        name="mlstm_group",
    )(z, z, z, z, gi, gf, bi, bf, norm_g.reshape(1, -1))


def _compress_kernel(x_ref, pe_ref, w1_ref, w2_ref, o_ref, sh_s):
    half = CMP_STRIDE * HEAD_DIM
    ngroups = x_ref.shape[0]
    x = x_ref[...].astype(F32)
    lo = jnp.dot((x + pe_ref[:, :half]).astype(BF16), w1_ref[:half, :], preferred_element_type=F32)
    hi = jnp.dot((x + pe_ref[:, half:]).astype(BF16), w1_ref[half:, :], preferred_element_type=F32)
    sh_s[0:ngroups, :] = hi
    sh_s[ngroups:ngroups + SUBLANES_F32, :] = jnp.zeros((SUBLANES_F32, CMP_HIDDEN), F32)
    pre = lo + sh_s[1:ngroups + 1, :]
    act = jax.nn.gelu(pre, approximate=True)
    o_ref[0] = jnp.dot(act.astype(BF16), w2_ref[...], preferred_element_type=F32).astype(o_ref.dtype)


def compress(grp, first_head, pe, w1, w2, layer, batch, seq):
    nslab, ngroups, gw = batch * NSA_KV_HEADS, seq // CMP_STRIDE, CMP_STRIDE * HEAD_DIM
    return pl.pallas_call(
        _compress_kernel,
        out_shape=jax.ShapeDtypeStruct((nslab, ngroups, HEAD_DIM), BF16),
        grid=(nslab,),
        in_specs=[pl.BlockSpec((ngroups, gw), lambda s: (s // NSA_KV_HEADS, first_head + s % NSA_KV_HEADS)),
                  pl.BlockSpec((1, 2 * gw), lambda s: (0, 0)),
                  pl.BlockSpec((None, 2 * gw, CMP_HIDDEN), lambda s: (layer, 0, 0)),
                  pl.BlockSpec((None, CMP_HIDDEN, HEAD_DIM), lambda s: (layer, 0, 0))],
        out_specs=pl.BlockSpec((1, ngroups, HEAD_DIM), lambda s: (s, 0, 0)),
        scratch_shapes=[pltpu.VMEM((ngroups + SUBLANES_F32, CMP_HIDDEN), F32)],
        compiler_params=_cparams("parallel"),
        name="nsa_compress",
    )(grp, pe.reshape(1, -1), w1, w2)


def _cmp_select_kernel(nsel, q_ref, kc_ref, vc_ref, ov_ref, o_ref, sel_ref):
    tq = q_ref.shape[0]
    s0 = pl.program_id(2) * tq
    kc = kc_ref[0]
    vc = vc_ref[0]
    ncp = kc.shape[0]
    pos = s0 + lax.broadcasted_iota(jnp.int32, (tq, ncp), 0)
    cidx = lax.broadcasted_iota(jnp.int32, (tq, ncp), 1)
    valid = cidx * CMP_STRIDE + (CMP_BLOCK - 1) <= pos
    any_valid = s0 + lax.broadcasted_iota(jnp.int32, (tq, 1), 0) >= CMP_BLOCK - 1
    sc = _nt_dot(_stack_heads(q_ref), kc)
    sc = jnp.where(valid[None], sc.reshape(NSA_GROUP, tq, ncp), NEG)
    e = jnp.where(valid[None], jnp.exp(sc - jnp.max(sc, axis=-1, keepdims=True)), 0.0)
    denom = jnp.sum(e, axis=-1, keepdims=True)
    p = e / jnp.where(any_valid[None], denom, 1.0)
    o = jnp.dot(p.reshape(NSA_GROUP * tq, ncp).astype(BF16), vc, preferred_element_type=F32)
    p_sum = p[0]
    for g in range(NSA_GROUP):
        o_ref[:, g * HEAD_DIM:(g + 1) * HEAD_DIM] = o[g * tq:(g + 1) * tq, :].astype(o_ref.dtype)
        if g:
            p_sum = p_sum + p[g]
    imp = jnp.dot(p_sum, ov_ref[...], preferred_element_type=F32, precision=lax.Precision.HIGHEST)
    jb = cidx
    cur = pos // SEL_BLOCK
    forced = (jb == 0) | (jb == cur) | (jb == cur - 1)
    imp = jnp.where(forced, FORCE_SCORE, imp)
    imp = jnp.where(jb <= cur, imp, -1.0)
    imp_t = imp.T[:nsel, :]
    jb_t = lax.broadcasted_iota(jnp.int32, (nsel, tq), 0)
    rank = jnp.zeros((nsel, tq), F32)
    for j2 in range(nsel):
        rowv = imp_t[j2:j2 + 1, :]
        before = (rowv > imp_t) | ((rowv == imp_t) & (jb_t > j2))
        rank = rank + jnp.where(before, 1.0, 0.0)
    sel_t = jnp.where(rank < float(SEL_TOPK), 1.0, 0.0)
    sel_t = jnp.concatenate([sel_t, jnp.zeros((ncp - nsel, tq), F32)], axis=0)
    sel_ref[...] = sel_t.T.astype(sel_ref.dtype)


def cmp_select(z, kcmp, vcmp, overlap, batch, seq):
    m = z.shape[0]
    tq = CMP_TQ
    nq = seq // tq
    ncp = kcmp.shape[1]
    gw = NSA_GROUP * HEAD_DIM
    return pl.pallas_call(
        functools.partial(_cmp_select_kernel, seq // SEL_BLOCK),
        out_shape=(jax.ShapeDtypeStruct((m, NSA_W), BF16),
                   jax.ShapeDtypeStruct((m, NSA_KV_HEADS * LANES), BF16)),
        grid=(batch, NSA_KV_HEADS, nq),
        in_specs=[pl.BlockSpec((tq, gw), lambda b, h, i: (b * nq + i, Z_NQ // gw + h)),
                  pl.BlockSpec((1, ncp, HEAD_DIM), lambda b, h, i: (b * NSA_KV_HEADS + h, 0, 0)),
                  pl.BlockSpec((1, ncp, HEAD_DIM), lambda b, h, i: (b * NSA_KV_HEADS + h, 0, 0)),
                  pl.BlockSpec((ncp, LANES), lambda b, h, i: (0, 0))],
        out_specs=(pl.BlockSpec((tq, gw), lambda b, h, i: (b * nq + i, h)),
                   pl.BlockSpec((tq, LANES), lambda b, h, i: (b * nq + i, h))),
        compiler_params=_cparams("parallel", "parallel", "parallel"),
        name="nsa_cmp_select",
    )(z, kcmp, vcmp, overlap)


def _sel_attn_kernel(q_ref, k_ref, v_ref, oh_ref, sel_ref, o_ref, qx_s, m_s, acc_s, s0_s, s1_s):
    tq, tk = ATTN_TQ, ATTN_TK
    i = pl.program_id(2)
    penalty = ((sel_ref[...].astype(F32) - 1.0) * (-NEG)).astype(BF16)
    for g in range(NSA_GROUP):
        cs = slice(g * HEAD_DIM, (g + 1) * HEAD_DIM)
        qx_s[g * tq:(g + 1) * tq, :] = jnp.concatenate([q_ref[:, cs], penalty], axis=1)
    m_s[...] = jnp.full_like(m_s, NEG)
    acc_s[...] = jnp.zeros_like(acc_s)
    ones = jnp.ones((tk, LANES), BF16)

    def scores(t, s_buf):
        k0 = pl.multiple_of(t * tk, tk)
        kx = jnp.concatenate([k_ref[pl.ds(k0, tk), :], oh_ref[pl.ds(k0, tk), :]], axis=1)
        s_buf[...] = _nt_dot(qx_s[...], kx)

    def update(t, s_buf, diagonal):
        k0 = pl.multiple_of(t * tk, tk)
        vx = jnp.concatenate([v_ref[pl.ds(k0, tk), :], ones], axis=1)
        s = s_buf[...]
        if diagonal:
            row = lax.broadcasted_iota(jnp.int32, (tq, tk), 0)
            col = lax.broadcasted_iota(jnp.int32, (tq, tk), 1)
            causal = jnp.where(col <= row, 0.0, NEG)
            s = (s.reshape(NSA_GROUP, tq, tk) + causal[None]).reshape(NSA_GROUP * tq, tk)
        m_old = m_s[...]
        m_new = jnp.maximum(m_old, jnp.max(s, axis=-1, keepdims=True))
        alpha = jnp.exp(m_old - m_new)
        p = jnp.exp((s - jnp.concatenate([m_new] * (tk // LANES), axis=1)).astype(BF16))
        acc_s[...] = (jnp.concatenate([alpha, alpha], axis=1) * acc_s[...]
                      + jnp.dot(p, vx, preferred_element_type=F32))
        m_s[...] = m_new

    scores(0, s0_s)

    def pair(p, carry):
        t = 2 * p
        scores(t + 1, s1_s)
        update(t, s0_s, False)
        scores(t + 2, s0_s)
        update(t + 1, s1_s, False)
        return carry

    lax.fori_loop(0, i // 2, pair, 0)

    @pl.when(i % 2 == 0)
    def _():
        update(i, s0_s, True)

    @pl.when(i % 2 == 1)
    def _():
        scores(i, s1_s)
        update(i - 1, s0_s, False)
        update(i, s1_s, True)
    for g in range(NSA_GROUP):
        cs = slice(g * HEAD_DIM, (g + 1) * HEAD_DIM)
        acc = acc_s[g * tq:(g + 1) * tq, :]
        o_ref[:, cs] = (acc[:, :HEAD_DIM] / acc[:, HEAD_DIM:]).astype(o_ref.dtype)


def _kv_spec(seq, col0):
    return pl.BlockSpec((seq, HEAD_DIM), lambda b, h, i: (b, col0 // HEAD_DIM + h))


def sel_attn(z, onehot, sel, batch, seq):
    assert ATTN_TQ == ATTN_TK
    m = z.shape[0]
    tq = ATTN_TQ
    nq = seq // tq
    gw = NSA_GROUP * HEAD_DIM
    return pl.pallas_call(
        _sel_attn_kernel,
        out_shape=jax.ShapeDtypeStruct((m, NSA_W), BF16),
        grid=(batch, NSA_KV_HEADS, nq),
        in_specs=[pl.BlockSpec((tq, gw), lambda b, h, i: (b * nq + i, Z_NQ // gw + h)),
                  _kv_spec(seq, Z_KS), _kv_spec(seq, Z_VS),
                  pl.BlockSpec((seq, LANES), lambda b, h, i: (0, 0)),
                  pl.BlockSpec((tq, LANES), lambda b, h, i: (b * nq + i, h))],
        out_specs=pl.BlockSpec((tq, gw), lambda b, h, i: (b * nq + i, h)),
        scratch_shapes=[pltpu.VMEM((NSA_GROUP * tq, 2 * LANES), BF16),
                        pltpu.VMEM((NSA_GROUP * tq, LANES), F32),
                        pltpu.VMEM((NSA_GROUP * tq, 2 * LANES), F32),
                        pltpu.VMEM((NSA_GROUP * tq, ATTN_TK), F32),
                        pltpu.VMEM((NSA_GROUP * tq, ATTN_TK), F32)],
        compiler_params=_cparams("parallel", "parallel", "arbitrary"),
        name="nsa_sel_attn",
    )(z, z, z, onehot, sel)


def _win_attn_kernel(q_ref, k_ref, v_ref, o_ref):
    tq, nk = ATTN_TQ, WIN_KEYS
    s0 = pl.program_id(2) * tq
    start = pl.multiple_of(jnp.maximum(s0 - WINDOW, 0), tq)
    kw = k_ref[pl.ds(start, nk), :]
    vx = jnp.concatenate([v_ref[pl.ds(start, nk), :], jnp.ones((nk, LANES), BF16)], axis=1)
    qpos = s0 + lax.broadcasted_iota(jnp.int32, (tq, nk), 0)
    kpos = start + lax.broadcasted_iota(jnp.int32, (tq, nk), 1)
    diff = qpos - kpos
    bias = jnp.where((diff >= 0) & (diff < WINDOW), 0.0, NEG)
    for g in range(NSA_GROUP):
        cs = slice(g * HEAD_DIM, (g + 1) * HEAD_DIM)
        s = _nt_dot(q_ref[:, cs], kw) + bias
        p = jnp.exp((s - jnp.max(s, axis=-1, keepdims=True)).astype(BF16))
        a = jnp.dot(p, vx, preferred_element_type=F32)
        o_ref[:, cs] = (a[:, :HEAD_DIM] / a[:, HEAD_DIM:]).astype(o_ref.dtype)


def win_attn(z, batch, seq):
    m = z.shape[0]
    tq = ATTN_TQ
    nq = seq // tq
    gw = NSA_GROUP * HEAD_DIM
    return pl.pallas_call(
        _win_attn_kernel,
        out_shape=jax.ShapeDtypeStruct((m, NSA_W), BF16),
        grid=(batch, NSA_KV_HEADS, nq),
        in_specs=[pl.BlockSpec((tq, gw), lambda b, h, i: (b * nq + i, Z_NQ // gw + h)),
                  _kv_spec(seq, Z_KW), _kv_spec(seq, Z_VW)],
        out_specs=pl.BlockSpec((tq, gw), lambda b, h, i: (b * nq + i, h)),
        compiler_params=_cparams("parallel", "parallel", "parallel"),
        name="nsa_win_attn",
    )(z, z, z)


def _nsa_combine_kernel(oc_ref, os_ref, ow_ref, zg_ref, ex_ref, nrm_ref, o_ref, mix_s):
    tm = oc_ref.shape[0]
    gates = jax.nn.sigmoid(zg_ref[...])
    g_hi = gates.astype(BF16)
    g_lo = (gates - g_hi.astype(F32)).astype(BF16)
    mix = None
    for br, src in enumerate((oc_ref, os_ref, ow_ref)):
        spread = (jnp.dot(g_hi, ex_ref[br], preferred_element_type=F32)
                  + jnp.dot(g_lo, ex_ref[br], preferred_element_type=F32))
        term = spread * src[...].astype(F32)
        mix = term if mix is None else mix + term
    mix_s[...] = mix
    inv = lax.rsqrt(jnp.mean(mix * mix, axis=-1, keepdims=True) + NORM_EPS)
    o_ref[...] = (mix_s[...] * inv * nrm_ref[...]).astype(o_ref.dtype)


def _gate_spread():
    col = jnp.arange(LANES)[None, :, None]
    head = (jnp.arange(NSA_W) // HEAD_DIM)[None, None, :]
    br = jnp.arange(3)[:, None, None]
    return (col == ZG_NG + 3 * head + br).astype(BF16)


def nsa_combine(o_cmp, o_slc, o_win, zg, norm_g):
    m = o_cmp.shape[0]
    tm = ROW_TM
    ospec = pl.BlockSpec((tm, NSA_W), lambda i: (i, 0))
    return pl.pallas_call(
        _nsa_combine_kernel,
        out_shape=jax.ShapeDtypeStruct((m, NSA_W), BF16),
        grid=(m // tm,),
        in_specs=[ospec, ospec, ospec, pl.BlockSpec((tm, LANES), lambda i: (i, 0)),
                  pl.BlockSpec((3, LANES, NSA_W), lambda i: (0, 0, 0)),
                  pl.BlockSpec((1, NSA_W), lambda i: (0, 0))],
        out_specs=ospec,
        scratch_shapes=[pltpu.VMEM((tm, NSA_W), F32)],
        compiler_params=_cparams("parallel"),
        name="nsa_combine",
    )(o_cmp, o_slc, o_win, zg, _gate_spread(), norm_g.reshape(1, -1))


def _ffn_up_kernel(tiles_per_seq, x_ref, xp_ref, g_ref, wg_ref, wu_ref, cwg_ref, cwu_ref,
                   cbg_ref, cbu_ref, o_ref, h_s, ug_s, uu_s):
    tm = x_ref.shape[0]

    @pl.when(pl.program_id(1) == 0)
    def _():
        _fill_halo_h(tiles_per_seq, x_ref, xp_ref, g_ref, h_s)

    ug_s[...] = jnp.dot(h_s[...], wg_ref[...], preferred_element_type=F32)
    uu_s[...] = jnp.dot(h_s[...], wu_ref[...], preferred_element_type=F32)
    gate = _causal_conv(ug_s, cwg_ref, cbg_ref, FFN_CONV, tm)
    up = _causal_conv(uu_s, cwu_ref, cbu_ref, FFN_CONV, tm)
    o_ref[...] = (gate * jax.nn.sigmoid(gate) * up).astype(o_ref.dtype)


def ffn_up(x, g, w_up, layer, conv_w, conv_b, seq):
    m, k = x.shape
    tm, tn, halo = MM_TM, MM_TN, CONV_HALO
    ntile = D_FF // tn
    cb = conv_b.reshape(1, -1)
    return pl.pallas_call(
        functools.partial(_ffn_up_kernel, seq // tm),
        out_shape=jax.ShapeDtypeStruct((m, D_FF), BF16),
        grid=(m // tm, ntile),
        in_specs=[
            pl.BlockSpec((tm, k), lambda i, j: (i, 0)),
            pl.BlockSpec((halo, k), lambda i, j: (jnp.maximum(i * (tm // halo) - 1, 0), 0)),
            pl.BlockSpec((1, k), lambda i, j: (0, 0)),
            pl.BlockSpec((None, k, tn), lambda i, j: (layer, 0, j)),
            pl.BlockSpec((None, k, tn), lambda i, j: (layer, 0, ntile + j)),
            pl.BlockSpec((FFN_CONV, tn), lambda i, j: (0, j)),
            pl.BlockSpec((FFN_CONV, tn), lambda i, j: (0, ntile + j)),
            pl.BlockSpec((1, tn), lambda i, j: (0, j)),
            pl.BlockSpec((1, tn), lambda i, j: (0, ntile + j)),
        ],
        out_specs=pl.BlockSpec((tm, tn), lambda i, j: (i, j)),
        scratch_shapes=[pltpu.VMEM((tm + halo, k), BF16),
                        pltpu.VMEM((tm + halo, tn), F32),
                        pltpu.VMEM((tm + halo, tn), F32)],
        compiler_params=_cparams("parallel", "arbitrary"),
        name="ffn_up",
    )(x, x, g.reshape(1, k), w_up, w_up, conv_w, conv_w, cb, cb)


def _rmsnorm_kernel(x_ref, g_ref, o_ref):
    o_ref[...] = _rms(x_ref[...], g_ref[...])


def rmsnorm(x, g):
    m, k = x.shape
    tm = ROW_TM
    return pl.pallas_call(
        _rmsnorm_kernel,
        out_shape=jax.ShapeDtypeStruct((m, k), F32),
        grid=(m // tm,),
        in_specs=[pl.BlockSpec((tm, k), lambda i: (i, 0)), pl.BlockSpec((1, k), lambda i: (0, 0))],
        out_specs=pl.BlockSpec((tm, k), lambda i: (i, 0)),
        compiler_params=_cparams("parallel"),
        name="final_rmsnorm",
    )(x, g.reshape(1, k))


def _gate_weight(w_in):
    gates = jnp.concatenate([w_in[..., W_IN_MI:W_IN_MI + 2 * MLSTM_HEADS],
                             w_in[..., W_IN_NG:W_IN_NG + 3 * NSA_HEADS]], axis=-1)
    return jnp.pad(gates, ((0, 0), (0, 0), (0, LANES - gates.shape[-1]))).astype(BF16)


def _rope_tables(seq):
    pos = jnp.arange(seq, dtype=F32)
    inv = ROPE_THETA ** (-jnp.arange(0, ROPE_DIM, 2, dtype=F32) / ROPE_DIM)
    ang = pos[:, None] * inv[None, :]
    cos, sin = jnp.cos(ang), jnp.sin(ang)
    rest = HEAD_DIM - ROPE_DIM
    cosf = jnp.concatenate([cos, cos, jnp.ones((seq, rest), F32)], axis=1)
    sinf = jnp.concatenate([-sin, sin, jnp.zeros((seq, rest), F32)], axis=1)
    return cosf, sinf


def _overlap_matrix(seq, ncmp_pad):
    nsel = seq // SEL_BLOCK
    cstart = jnp.arange(ncmp_pad) * CMP_STRIDE
    jb = jnp.arange(LANES)
    ov = ((cstart[:, None] < (jb[None, :] + 1) * SEL_BLOCK)
          & (cstart[:, None] + CMP_BLOCK > jb[None, :] * SEL_BLOCK) & (jb[None, :] < nsel))
    return ov.astype(F32)


def _block_onehot(seq):
    blk = jnp.arange(seq)[:, None] // SEL_BLOCK
    return (blk == jnp.arange(LANES)[None, :]).astype(BF16)


def _layer(x, p, wts, layer, cosf, sinf, overlap, onehot, batch, seq):
    z, zg, grp = proj_in(x, p["attn_norm"], wts["w_in"], wts["w_nsa"], wts["w_gate"], layer,
                         p["qk_conv_w"], p["qk_conv_b"], cosf, sinf, seq)

    nchunks = seq // MLSTM_CHUNK
    gate_rows = lambda c0: (zg[:, c0:c0 + MLSTM_HEADS].reshape(batch, seq, MLSTM_HEADS)
                            .transpose(0, 2, 1).reshape(batch, MLSTM_HEADS, nchunks, MLSTM_CHUNK))
    lanes = lambda v: jnp.broadcast_to(v.reshape(MLSTM_HEADS, 1, 1), (MLSTM_HEADS, 1, MLSTM_CHUNK))
    mix_a = mlstm_group(z, gate_rows(ZG_MI), gate_rows(ZG_MF), lanes(p["i_bias"]),
                        lanes(p["f_bias"]), p["mlstm_norm"], batch, seq)

    kcmp = compress(grp, 0, p["cmp_pe_k"], wts["cmp_w1_k"], wts["cmp_w2_k"], layer, batch, seq)
    vcmp = compress(grp, NSA_KV_HEADS, p["cmp_pe_v"], wts["cmp_w1_v"], wts["cmp_w2_v"], layer, batch, seq)
    o_cmp, sel = cmp_select(z, kcmp, vcmp, overlap, batch, seq)
    o_slc = sel_attn(z, onehot, sel, batch, seq)
    o_win = win_attn(z, batch, seq)
    mix_b = nsa_combine(o_cmp, o_slc, o_win, zg, p["nsa_norm"])

    x = matmul_res([mix_a, mix_b], wts["w_out"], layer, x, OUT_TM, D_MODEL)

    act = ffn_up(x, p["ffn_norm"], wts["w_up"], layer, p["ffn_conv_w"], p["ffn_conv_b"], seq)
    x = matmul_res([act], wts["w_down"], layer, x, MM_TM, DOWN_TN)
    return x


_LAYER_PARAMS = ("attn_norm", "w_in", "qk_conv_w", "qk_conv_b", "i_bias", "f_bias", "mlstm_norm",
                 "cmp_pe_k", "cmp_pe_v", "cmp_w1_k", "cmp_w2_k", "cmp_w1_v", "cmp_w2_v", "nsa_norm",
                 "w_out", "ffn_norm", "w_up", "ffn_conv_w", "ffn_conv_b", "w_down")


def kernel(x, attn_norm, w_in, qk_conv_w, qk_conv_b, i_bias, f_bias, mlstm_norm, cmp_pe_k, cmp_pe_v,
           cmp_w1_k, cmp_w2_k, cmp_w1_v, cmp_w2_v, nsa_norm, w_out, ffn_norm, w_up, ffn_conv_w,
           ffn_conv_b, w_down, final_norm):
    stacked = dict(zip(_LAYER_PARAMS, (attn_norm, w_in, qk_conv_w, qk_conv_b, i_bias, f_bias,
                                       mlstm_norm, cmp_pe_k, cmp_pe_v, cmp_w1_k, cmp_w2_k, cmp_w1_v,
                                       cmp_w2_v, nsa_norm, w_out, ffn_norm, w_up, ffn_conv_w,
                                       ffn_conv_b, w_down)))
    batch, seq, d = x.shape
    assert d == D_MODEL and seq % MM_TM == 0 and seq % ATTN_TQ == 0 and seq >= WIN_KEYS
    assert seq // CMP_STRIDE == LANES and seq // SEL_BLOCK <= LANES
    cosf, sinf = _rope_tables(seq)
    overlap = _overlap_matrix(seq, seq // CMP_STRIDE)
    onehot = _block_onehot(seq)
    big = ("w_in", "w_out", "w_up", "w_down", "cmp_w1_k", "cmp_w2_k", "cmp_w1_v", "cmp_w2_v")
    wts = {k: stacked[k].astype(BF16) for k in big}
    wts["w_nsa"] = wts["w_in"][..., W_IN_NSA[0]:W_IN_NSA[1]]
    wts["w_gate"] = _gate_weight(wts["w_in"])
    xf = x.reshape(batch * seq, d)
    for layer in range(DEPTH):
        p = {k: v[layer] for k, v in stacked.items() if k not in big}
        xf = _layer(xf, p, wts, layer, cosf, sinf, overlap, onehot, batch, seq)
    return rmsnorm(xf, final_norm).reshape(batch, seq, d)
```

```python
import functools

import jax
import jax.numpy as jnp
from jax import lax
from jax.experimental import pallas as pl
from jax.experimental.pallas import tpu as pltpu

D_MODEL = 2048
DEPTH = 2
MLSTM_HEADS = 8
HEAD_DIM = 128
MLSTM_W = MLSTM_HEADS * HEAD_DIM
QK_CONV = 4
NSA_HEADS = 8
NSA_KV_HEADS = 2
NSA_GROUP = NSA_HEADS // NSA_KV_HEADS
NSA_W = NSA_HEADS * HEAD_DIM
NSA_KV_W = NSA_KV_HEADS * HEAD_DIM
CMP_BLOCK = 32
CMP_STRIDE = 16
CMP_HIDDEN = 256
SEL_BLOCK = 64
SEL_TOPK = 16
WINDOW = 512
ROPE_THETA = 500000.0
ROPE_DIM = HEAD_DIM // 4
D_FF = 5632
FFN_CONV = 3
NORM_EPS = 1e-6
NEG = -1e30
FORCE_SCORE = 1e4
ATTN_SCALE = HEAD_DIM ** -0.5

LANES = 128
SUBLANES_F32 = 8
SUBLANES_BF16 = 16
VMEM_LIMIT_BYTES = 48 * 1024 * 1024

MM_TM = 1024
MM_TN = 512
EPI_ROWS = 128
DOWN_TN = 512
OUT_TM = 512
MLSTM_CHUNK = 256
MLSTM_HEADS_PER_STEP = 4
ATTN_TQ = 256
CMP_TQ = 512
ATTN_TK = 256
WIN_KEYS = WINDOW + ATTN_TQ
ROW_TM = 512
CONV_HALO = SUBLANES_BF16

BF16 = jnp.bfloat16
F32 = jnp.float32

W_IN_MI = 4 * MLSTM_W
W_IN_NSA = (W_IN_MI + 2 * MLSTM_HEADS, W_IN_MI + 2 * MLSTM_HEADS + NSA_W + 6 * NSA_KV_W)
W_IN_NG = W_IN_NSA[1]
Z_MQ, Z_MK, Z_MV, Z_MO, Z_NQ = (i * MLSTM_W for i in range(5))
Z_KC, Z_VC, Z_KS, Z_VS, Z_KW, Z_VW = (Z_NQ + NSA_W + i * NSA_KV_W for i in range(6))
Z_W = Z_VW + NSA_KV_W
ZG_MI, ZG_MF, ZG_NG = 0, MLSTM_HEADS, 2 * MLSTM_HEADS


def _cparams(*sem):
    return pltpu.CompilerParams(dimension_semantics=sem, vmem_limit_bytes=VMEM_LIMIT_BYTES)


def _rms(x, gain):
    return x * lax.rsqrt(jnp.mean(x * x, axis=-1, keepdims=True) + NORM_EPS) * gain


def _nt_dot(a, b):
    return lax.dot_general(a, b, (((1,), (1,)), ((), ())), preferred_element_type=F32)


def _stack_heads(q_ref):
    return jnp.concatenate([q_ref[:, g * HEAD_DIM:(g + 1) * HEAD_DIM] for g in range(NSA_GROUP)], axis=0)


def _fill_halo_h(tiles_per_seq, x_ref, xp_ref, g_ref, h_s):
    tm = x_ref.shape[0]
    first = pl.program_id(0) % tiles_per_seq == 0
    hp = _rms(xp_ref[...], g_ref[...])
    h_s[0:CONV_HALO, :] = jnp.where(first, 0.0, hp).astype(h_s.dtype)
    h_s[CONV_HALO:CONV_HALO + tm, :] = _rms(x_ref[...], g_ref[...]).astype(h_s.dtype)


def _causal_conv(u_s, cw_ref, cb_ref, taps, tm):
    u = u_s[...]
    acc = jnp.broadcast_to(cb_ref[...], (tm, u_s.shape[1]))
    for j in range(taps):
        back = taps - 1 - j
        shifted = pltpu.roll(u, back, axis=0) if back else u
        acc = acc + shifted[CONV_HALO:CONV_HALO + tm, :] * cw_ref[j:j + 1, :]
    return acc


def _rope(x, cosf, sinf, lane):
    rot = jnp.where(lane < ROPE_DIM // 2, pltpu.roll(x, LANES - ROPE_DIM // 2, axis=1),
                    pltpu.roll(x, ROPE_DIM // 2, axis=1))
    return x * cosf + rot * sinf


def _proj_in_kernel(tiles_per_seq, x_ref, xp_ref, g_ref, w_ref, wn_ref, wg_ref, cw_ref, cb_ref, cos_ref, sin_ref,
                    o_ref, zg_ref, grp_ref, h_s, u_s, y_s):
    tm, tn = o_ref.shape
    j = pl.program_id(1)
    j_k, j_v, j_nq, j_nkv = Z_MK // tn, Z_MV // tn, Z_NQ // tn, Z_KC // tn

    @pl.when(j == 0)
    def _():
        _fill_halo_h(tiles_per_seq, x_ref, xp_ref, g_ref, h_s)
        zg_ref[...] = jnp.dot(h_s[CONV_HALO:CONV_HALO + tm, :], wg_ref[...], preferred_element_type=F32)
        grp_ref[...] = jnp.zeros_like(grp_ref)

    @pl.when(j < j_v)
    def _():
        u_s[...] = jnp.dot(h_s[...], w_ref[...], preferred_element_type=F32)
        y = _causal_conv(u_s, cw_ref, cb_ref, QK_CONV, tm)
        scale = jnp.where(j >= j_k, ATTN_SCALE, 1.0)
        o_ref[...] = (y * jax.nn.sigmoid(y) * scale).astype(o_ref.dtype)

    @pl.when((j >= j_v) & (j < j_nq))
    def _():
        o_ref[...] = jnp.dot(h_s[CONV_HALO:CONV_HALO + tm, :], w_ref[...],
                             preferred_element_type=F32).astype(o_ref.dtype)

    @pl.when(j >= j_nq)
    def _():
        is_q = j < j_nkv
        for r in range(tm // EPI_ROWS):
            rs = slice(r * EPI_ROWS, (r + 1) * EPI_ROWS)
            cosf, sinf = cos_ref[rs, :], sin_ref[rs, :]
            lane = lax.broadcasted_iota(jnp.int32, cosf.shape, 1)
            u = jnp.dot(h_s[CONV_HALO + r * EPI_ROWS:CONV_HALO + (r + 1) * EPI_ROWS, :], wn_ref[...],
                        preferred_element_type=F32)
            for head in range(tn // HEAD_DIM):
                cs = slice(head * HEAD_DIM, (head + 1) * HEAD_DIM)
                xh = u[:, cs]
                roped = _rope(xh, cosf, sinf, lane)
                if head < NSA_KV_HEADS:
                    yh = roped * jnp.where(is_q, ATTN_SCALE, 1.0)
                else:
                    yh = jnp.where(is_q, roped * ATTN_SCALE, xh)
                o_ref[rs, cs] = yh.astype(o_ref.dtype)
                y_s[head, rs, :] = yh

        @pl.when(j == j_nkv)
        def _():
            ngrp = tm // CMP_STRIDE
            for head in range(tn // HEAD_DIM):
                for l in range(CMP_STRIDE):
                    col = (head * CMP_STRIDE + l) * HEAD_DIM
                    grp_ref[:, col:col + HEAD_DIM] = y_s[head, pl.ds(l, ngrp, stride=CMP_STRIDE),
                                                         :].astype(grp_ref.dtype)


def proj_in(x, g, w_in, w_nsa, w_gate, layer, conv_w, conv_b, cosf, sinf, seq):
    m, k = x.shape
    tm, tn, halo = MM_TM, MM_TN, CONV_HALO
    assert tn == 2 * NSA_KV_W and Z_MV % tn == 0 and Z_NQ % tn == 0 and Z_KC % tn == 0
    n_conv = Z_MV // tn
    n_m = Z_NQ // tn
    nper = seq // tm
    conv_tile = lambda j: jnp.minimum(j, n_conv - 1)
    return pl.pallas_call(
        functools.partial(_proj_in_kernel, seq // tm),
        out_shape=(jax.ShapeDtypeStruct((m, Z_W), BF16), jax.ShapeDtypeStruct((m, LANES), F32),
                   jax.ShapeDtypeStruct((m // CMP_STRIDE, tn * CMP_STRIDE), BF16)),
        grid=(m // tm, Z_W // tn),
        in_specs=[
            pl.BlockSpec((tm, k), lambda i, j: (i, 0)),
            pl.BlockSpec((halo, k), lambda i, j: (jnp.maximum(i * (tm // halo) - 1, 0), 0)),
            pl.BlockSpec((1, k), lambda i, j: (0, 0)),
            pl.BlockSpec((None, k, tn), lambda i, j: (layer, 0, jnp.minimum(j, n_m - 1))),
            pl.BlockSpec((None, k, tn), lambda i, j: (layer, 0, jnp.maximum(j - n_m, 0))),
            pl.BlockSpec((None, k, LANES), lambda i, j: (layer, 0, 0)),
            pl.BlockSpec((QK_CONV, tn), lambda i, j: (0, conv_tile(j))),
            pl.BlockSpec((1, tn), lambda i, j: (0, conv_tile(j))),
            pl.BlockSpec((tm, LANES), lambda i, j: (i % nper, 0)),
            pl.BlockSpec((tm, LANES), lambda i, j: (i % nper, 0)),
        ],
        out_specs=(pl.BlockSpec((tm, tn), lambda i, j: (i, j)),
                   pl.BlockSpec((tm, LANES), lambda i, j: (i, 0)),
                   pl.BlockSpec((tm // CMP_STRIDE, tn * CMP_STRIDE), lambda i, j: (i, 0))),
        scratch_shapes=[pltpu.VMEM((tm + halo, k), BF16), pltpu.VMEM((tm + halo, tn), F32),
                        pltpu.VMEM((tn // HEAD_DIM, tm, HEAD_DIM), F32)],
        compiler_params=_cparams("parallel", "arbitrary"),
        name="proj_in",
    )(x, x, g.reshape(1, k), w_in, w_nsa, w_gate, conv_w, conv_b.reshape(1, -1), cosf, sinf)


def _matmul_res_kernel(n_lhs, *refs):
    lhs = refs[:n_lhs]
    ws = refs[n_lhs:2 * n_lhs]
    res_ref, o_ref = refs[2 * n_lhs], refs[2 * n_lhs + 1]
    acc = res_ref[...]
    for a_ref, w_ref in zip(lhs, ws):
        acc = acc + jnp.dot(a_ref[...], w_ref[...], preferred_element_type=F32)
    o_ref[...] = acc


def matmul_res(lhs_list, w, layer, res, tm, tn):
    m, n = res.shape
    n_lhs = len(lhs_list)
    kb = w.shape[1] // n_lhs
    assert all(a.shape[1] == kb for a in lhs_list)
    in_specs = [pl.BlockSpec((tm, kb), lambda i, j: (i, 0)) for _ in lhs_list]
    in_specs += [pl.BlockSpec((None, kb, tn), lambda i, j, p=p: (layer, p, j)) for p in range(n_lhs)]
    in_specs += [pl.BlockSpec((tm, tn), lambda i, j: (i, j))]
    return pl.pallas_call(
        functools.partial(_matmul_res_kernel, n_lhs),
        out_shape=jax.ShapeDtypeStruct((m, n), F32),
        grid=(m // tm, n // tn),
        in_specs=in_specs,
        out_specs=pl.BlockSpec((tm, tn), lambda i, j: (i, j)),
        compiler_params=_cparams("parallel", "arbitrary"),
        name="matmul_res",
    )(*lhs_list, *([w] * n_lhs), res)


def _mlstm_kernel(seq, q_ref, k_ref, v_ref, zo_ref, gi_ref, gf_ref, bi_ref, bf_ref, nrm_ref, o_ref,
                  b_s, i_s, cx_s, m_s):
    L = MLSTM_CHUNK
    hps = MLSTM_HEADS_PER_STEP
    nchunks = seq // L

    lane = lax.broadcasted_iota(jnp.int32, (nchunks, L), 1)
    for hh in range(hps):
        i_s[hh] = gi_ref[0, hh] + bi_ref[hh]
        fpre = gf_ref[0, hh] + bf_ref[hh]
        csum = jnp.minimum(fpre, 0.0) - jnp.log1p(jnp.exp(-jnp.abs(fpre)))
        for sh in (1 << e for e in range(L.bit_length() - 1)):
            csum = csum + jnp.where(lane >= sh, pltpu.roll(csum, sh, axis=1), 0.0)
        b_s[hh] = csum

    cx_s[...] = jnp.zeros_like(cx_s)
    m_s[...] = jnp.zeros_like(m_s)

    row = lax.broadcasted_iota(jnp.int32, (L, L), 0)
    col = lax.broadcasted_iota(jnp.int32, (L, L), 1)
    eye = row == col
    causal = col <= row
    ones = jnp.ones((L, LANES), BF16)

    def chunk_body(c, carry):
        r0 = pl.multiple_of(c * L, L)
        for hh in range(hps):
            cs = slice(hh * HEAD_DIM, (hh + 1) * HEAD_DIM)
            qc = q_ref[pl.ds(r0, L), cs]
            kc = k_ref[pl.ds(r0, L), cs]
            vx = jnp.concatenate([v_ref[pl.ds(r0, L), cs], ones], axis=1)
            b_row = b_s[hh, pl.ds(c, 1), :]
            u_row = i_s[hh, pl.ds(c, 1), :] - b_row
            b_col = jnp.sum(jnp.where(eye, b_row, 0.0), axis=-1, keepdims=True)
            m_prev = m_s[hh]
            cx = cx_s[hh]

            umat = jnp.where(causal, u_row, -jnp.inf)
            m_col = jnp.maximum(m_prev, jnp.max(umat, axis=-1, keepdims=True))
            w_inter = jnp.exp(m_prev - m_col)
            s_qk = _nt_dot(qc, kc) * jnp.exp(umat - m_col)
            nd = (w_inter * jnp.dot(qc, cx.astype(BF16), preferred_element_type=F32)
                  + jnp.dot(s_qk.astype(BF16), vx, preferred_element_type=F32))
            floor = jnp.exp(-(b_col + m_col))
            h = nd[:, :HEAD_DIM] / jnp.maximum(jnp.abs(nd[:, HEAD_DIM:]), floor)

            m_last = m_col[L - 1:L, :]
            kw_t = kc.astype(F32).T * jnp.exp(u_row - m_last)
            cx_s[hh] = (jnp.exp(m_prev - m_last) * cx
                        + jnp.dot(kw_t.astype(BF16), vx, preferred_element_type=F32))
            m_s[hh] = b_row[:, L - 1:L] + m_last

            hn = _rms(h, nrm_ref[:, cs])
            gate = jax.nn.sigmoid(zo_ref[pl.ds(r0, L), cs].astype(F32))
            o_ref[pl.ds(r0, L), cs] = (hn * gate).astype(o_ref.dtype)
        return carry

    lax.fori_loop(0, nchunks, chunk_body, 0)


def mlstm_group(z, gi, gf, bi, bf, norm_g, batch, seq):
    hps = MLSTM_HEADS_PER_STEP
    wblk = hps * HEAD_DIM
    nchunks = seq // MLSTM_CHUNK
    nhb = MLSTM_HEADS // hps
    zspec = lambda col0: pl.BlockSpec((seq, wblk), lambda b, h: (b, col0 // wblk + h))
    gspec = pl.BlockSpec((1, hps, nchunks, MLSTM_CHUNK), lambda b, h: (b, h, 0, 0))
    bspec = pl.BlockSpec((hps, 1, MLSTM_CHUNK), lambda b, h: (h, 0, 0))
    return pl.pallas_call(
        functools.partial(_mlstm_kernel, seq),
        out_shape=jax.ShapeDtypeStruct((batch * seq, MLSTM_W), BF16),
        grid=(batch, nhb),
        in_specs=[zspec(Z_MQ), zspec(Z_MK), zspec(Z_MV), zspec(Z_MO), gspec, gspec, bspec, bspec,
                  pl.BlockSpec((1, wblk), lambda b, h: (0, h))],
        out_specs=pl.BlockSpec((seq, wblk), lambda b, h: (b, h)),
        scratch_shapes=[
            pltpu.VMEM((hps, nchunks, MLSTM_CHUNK), F32),
            pltpu.VMEM((hps, nchunks, MLSTM_CHUNK), F32),
            pltpu.VMEM((hps, HEAD_DIM, 2 * HEAD_DIM), F32),
            pltpu.VMEM((hps, 1, 1), F32),
        ],
        compiler_params=_cparams("parallel", "parallel"),
        name="mlstm_group",
    )(z, z, z, z, gi, gf, bi, bf, norm_g.reshape(1, -1))


def _compress_kernel(x_ref, pe_ref, w1_ref, w2_ref, o_ref, sh_s):
    half = CMP_STRIDE * HEAD_DIM
    ngroups = x_ref.shape[0]
    x = x_ref[...].astype(F32)
    lo = jnp.dot((x + pe_ref[:, :half]).astype(BF16), w1_ref[:half, :], preferred_element_type=F32)
    hi = jnp.dot((x + pe_ref[:, half:]).astype(BF16), w1_ref[half:, :], preferred_element_type=F32)
    sh_s[0:ngroups, :] = hi
    sh_s[ngroups:ngroups + SUBLANES_F32, :] = jnp.zeros((SUBLANES_F32, CMP_HIDDEN), F32)
    pre = lo + sh_s[1:ngroups + 1, :]
    act = jax.nn.gelu(pre, approximate=True)
    o_ref[0] = jnp.dot(act.astype(BF16), w2_ref[...], preferred_element_type=F32).astype(o_ref.dtype)


def compress(grp, first_head, pe, w1, w2, layer, batch, seq):
    nslab, ngroups, gw = batch * NSA_KV_HEADS, seq // CMP_STRIDE, CMP_STRIDE * HEAD_DIM
    return pl.pallas_call(
        _compress_kernel,
        out_shape=jax.ShapeDtypeStruct((nslab, ngroups, HEAD_DIM), BF16),
        grid=(nslab,),
        in_specs=[pl.BlockSpec((ngroups, gw), lambda s: (s // NSA_KV_HEADS, first_head + s % NSA_KV_HEADS)),
                  pl.BlockSpec((1, 2 * gw), lambda s: (0, 0)),
                  pl.BlockSpec((None, 2 * gw, CMP_HIDDEN), lambda s: (layer, 0, 0)),
                  pl.BlockSpec((None, CMP_HIDDEN, HEAD_DIM), lambda s: (layer, 0, 0))],
        out_specs=pl.BlockSpec((1, ngroups, HEAD_DIM), lambda s: (s, 0, 0)),
        scratch_shapes=[pltpu.VMEM((ngroups + SUBLANES_F32, CMP_HIDDEN), F32)],
        compiler_params=_cparams("parallel"),
        name="nsa_compress",
    )(grp, pe.reshape(1, -1), w1, w2)


def _cmp_select_kernel(nsel, q_ref, kc_ref, vc_ref, ov_ref, o_ref, sel_ref):
    tq = q_ref.shape[0]
    s0 = pl.program_id(2) * tq
    kc = kc_ref[0]
    vc = vc_ref[0]
    ncp = kc.shape[0]
    pos = s0 + lax.broadcasted_iota(jnp.int32, (tq, ncp), 0)
    cidx = lax.broadcasted_iota(jnp.int32, (tq, ncp), 1)
    valid = cidx * CMP_STRIDE + (CMP_BLOCK - 1) <= pos
    any_valid = s0 + lax.broadcasted_iota(jnp.int32, (tq, 1), 0) >= CMP_BLOCK - 1
    sc = _nt_dot(_stack_heads(q_ref), kc)
    sc = jnp.where(valid[None], sc.reshape(NSA_GROUP, tq, ncp), NEG)
    e = jnp.where(valid[None], jnp.exp(sc - jnp.max(sc, axis=-1, keepdims=True)), 0.0)
    denom = jnp.sum(e, axis=-1, keepdims=True)
    p = e / jnp.where(any_valid[None], denom, 1.0)
    o = jnp.dot(p.reshape(NSA_GROUP * tq, ncp).astype(BF16), vc, preferred_element_type=F32)
    p_sum = p[0]
    for g in range(NSA_GROUP):
        o_ref[:, g * HEAD_DIM:(g + 1) * HEAD_DIM] = o[g * tq:(g + 1) * tq, :].astype(o_ref.dtype)
        if g:
            p_sum = p_sum + p[g]
    imp = jnp.dot(p_sum, ov_ref[...], preferred_element_type=F32, precision=lax.Precision.HIGHEST)
    jb = cidx
    cur = pos // SEL_BLOCK
    forced = (jb == 0) | (jb == cur) | (jb == cur - 1)
    imp = jnp.where(forced, FORCE_SCORE, imp)
    imp = jnp.where(jb <= cur, imp, -1.0)
    imp_t = imp.T[:nsel, :]
    jb_t = lax.broadcasted_iota(jnp.int32, (nsel, tq), 0)
    rank = jnp.zeros((nsel, tq), F32)
    for j2 in range(nsel):
        rowv = imp_t[j2:j2 + 1, :]
        before = (rowv > imp_t) | ((rowv == imp_t) & (jb_t > j2))
        rank = rank + jnp.where(before, 1.0, 0.0)
    sel_t = jnp.where(rank < float(SEL_TOPK), 1.0, 0.0)
    sel_t = jnp.concatenate([sel_t, jnp.zeros((ncp - nsel, tq), F32)], axis=0)
    sel_ref[...] = sel_t.T.astype(sel_ref.dtype)


def cmp_select(z, kcmp, vcmp, overlap, batch, seq):
    m = z.shape[0]
    tq = CMP_TQ
    nq = seq // tq
    ncp = kcmp.shape[1]
    gw = NSA_GROUP * HEAD_DIM
    return pl.pallas_call(
        functools.partial(_cmp_select_kernel, seq // SEL_BLOCK),
        out_shape=(jax.ShapeDtypeStruct((m, NSA_W), BF16),
                   jax.ShapeDtypeStruct((m, NSA_KV_HEADS * LANES), BF16)),
        grid=(batch, NSA_KV_HEADS, nq),
        in_specs=[pl.BlockSpec((tq, gw), lambda b, h, i: (b * nq + i, Z_NQ // gw + h)),
                  pl.BlockSpec((1, ncp, HEAD_DIM), lambda b, h, i: (b * NSA_KV_HEADS + h, 0, 0)),
                  pl.BlockSpec((1, ncp, HEAD_DIM), lambda b, h, i: (b * NSA_KV_HEADS + h, 0, 0)),
                  pl.BlockSpec((ncp, LANES), lambda b, h, i: (0, 0))],
        out_specs=(pl.BlockSpec((tq, gw), lambda b, h, i: (b * nq + i, h)),
                   pl.BlockSpec((tq, LANES), lambda b, h, i: (b * nq + i, h))),
        compiler_params=_cparams("parallel", "parallel", "parallel"),
        name="nsa_cmp_select",
    )(z, kcmp, vcmp, overlap)


def _sel_attn_kernel(q_ref, k_ref, v_ref, oh_ref, sel_ref, o_ref, qx_s, m_s, acc_s, s0_s, s1_s):
    tq, tk = ATTN_TQ, ATTN_TK
    i = pl.program_id(2)
    penalty = ((sel_ref[...].astype(F32) - 1.0) * (-NEG)).astype(BF16)
    for g in range(NSA_GROUP):
        cs = slice(g * HEAD_DIM, (g + 1) * HEAD_DIM)
        qx_s[g * tq:(g + 1) * tq, :] = jnp.concatenate([q_ref[:, cs], penalty], axis=1)
    m_s[...] = jnp.full_like(m_s, NEG)
    acc_s[...] = jnp.zeros_like(acc_s)
    ones = jnp.ones((tk, LANES), BF16)

    def scores(t, s_buf):
        k0 = pl.multiple_of(t * tk, tk)
        kx = jnp.concatenate([k_ref[pl.ds(k0, tk), :], oh_ref[pl.ds(k0, tk), :]], axis=1)
        s_buf[...] = _nt_dot(qx_s[...], kx)

    def update(t, s_buf, diagonal):
        k0 = pl.multiple_of(t * tk, tk)
        vx = jnp.concatenate([v_ref[pl.ds(k0, tk), :], ones], axis=1)
        s = s_buf[...]
        if diagonal:
            row = lax.broadcasted_iota(jnp.int32, (tq, tk), 0)
            col = lax.broadcasted_iota(jnp.int32, (tq, tk), 1)
            causal = jnp.where(col <= row, 0.0, NEG)
            s = (s.reshape(NSA_GROUP, tq, tk) + causal[None]).reshape(NSA_GROUP * tq, tk)
        m_old = m_s[...]
        m_new = jnp.maximum(m_old, jnp.max(s, axis=-1, keepdims=True))
        alpha = jnp.exp(m_old - m_new)
        p = jnp.exp((s - jnp.concatenate([m_new] * (tk // LANES), axis=1)).astype(BF16))
        acc_s[...] = (jnp.concatenate([alpha, alpha], axis=1) * acc_s[...]
                      + jnp.dot(p, vx, preferred_element_type=F32))
        m_s[...] = m_new

    scores(0, s0_s)

    def pair(p, carry):
        t = 2 * p
        scores(t + 1, s1_s)
        update(t, s0_s, False)
        scores(t + 2, s0_s)
        update(t + 1, s1_s, False)
        return carry

    lax.fori_loop(0, i // 2, pair, 0)

    @pl.when(i % 2 == 0)
    def _():
        update(i, s0_s, True)

    @pl.when(i % 2 == 1)
    def _():
        scores(i, s1_s)
        update(i - 1, s0_s, False)
        update(i, s1_s, True)
    for g in range(NSA_GROUP):
        cs = slice(g * HEAD_DIM, (g + 1) * HEAD_DIM)
        acc = acc_s[g * tq:(g + 1) * tq, :]
        o_ref[:, cs] = (acc[:, :HEAD_DIM] / acc[:, HEAD_DIM:]).astype(o_ref.dtype)


def _kv_spec(seq, col0):
    return pl.BlockSpec((seq, HEAD_DIM), lambda b, h, i: (b, col0 // HEAD_DIM + h))


def sel_attn(z, onehot, sel, batch, seq):
    assert ATTN_TQ == ATTN_TK
    m = z.shape[0]
    tq = ATTN_TQ
    nq = seq // tq
    gw = NSA_GROUP * HEAD_DIM
    return pl.pallas_call(
        _sel_attn_kernel,
        out_shape=jax.ShapeDtypeStruct((m, NSA_W), BF16),
        grid=(batch, NSA_KV_HEADS, nq),
        in_specs=[pl.BlockSpec((tq, gw), lambda b, h, i: (b * nq + i, Z_NQ // gw + h)),
                  _kv_spec(seq, Z_KS), _kv_spec(seq, Z_VS),
                  pl.BlockSpec((seq, LANES), lambda b, h, i: (0, 0)),
                  pl.BlockSpec((tq, LANES), lambda b, h, i: (b * nq + i, h))],
        out_specs=pl.BlockSpec((tq, gw), lambda b, h, i: (b * nq + i, h)),
        scratch_shapes=[pltpu.VMEM((NSA_GROUP * tq, 2 * LANES), BF16),
                        pltpu.VMEM((NSA_GROUP * tq, LANES), F32),
                        pltpu.VMEM((NSA_GROUP * tq, 2 * LANES), F32),
                        pltpu.VMEM((NSA_GROUP * tq, ATTN_TK), F32),
                        pltpu.VMEM((NSA_GROUP * tq, ATTN_TK), F32)],
        compiler_params=_cparams("parallel", "parallel", "arbitrary"),
        name="nsa_sel_attn",
    )(z, z, z, onehot, sel)


def _win_attn_kernel(q_ref, k_ref, v_ref, o_ref):
    tq, nk = ATTN_TQ, WIN_KEYS
    s0 = pl.program_id(2) * tq
    start = pl.multiple_of(jnp.maximum(s0 - WINDOW, 0), tq)
    kw = k_ref[pl.ds(start, nk), :]
    vx = jnp.concatenate([v_ref[pl.ds(start, nk), :], jnp.ones((nk, LANES), BF16)], axis=1)
    qpos = s0 + lax.broadcasted_iota(jnp.int32, (tq, nk), 0)
    kpos = start + lax.broadcasted_iota(jnp.int32, (tq, nk), 1)
    diff = qpos - kpos
    bias = jnp.where((diff >= 0) & (diff < WINDOW), 0.0, NEG)
    for g in range(NSA_GROUP):
        cs = slice(g * HEAD_DIM, (g + 1) * HEAD_DIM)
        s = _nt_dot(q_ref[:, cs], kw) + bias
        p = jnp.exp((s - jnp.max(s, axis=-1, keepdims=True)).astype(BF16))
        a = jnp.dot(p, vx, preferred_element_type=F32)
        o_ref[:, cs] = (a[:, :HEAD_DIM] / a[:, HEAD_DIM:]).astype(o_ref.dtype)


def win_attn(z, batch, seq):
    m = z.shape[0]
    tq = ATTN_TQ
    nq = seq // tq
    gw = NSA_GROUP * HEAD_DIM
    return pl.pallas_call(
        _win_attn_kernel,
        out_shape=jax.ShapeDtypeStruct((m, NSA_W), BF16),
        grid=(batch, NSA_KV_HEADS, nq),
        in_specs=[pl.BlockSpec((tq, gw), lambda b, h, i: (b * nq + i, Z_NQ // gw + h)),
                  _kv_spec(seq, Z_KW), _kv_spec(seq, Z_VW)],
        out_specs=pl.BlockSpec((tq, gw), lambda b, h, i: (b * nq + i, h)),
        compiler_params=_cparams("parallel", "parallel", "parallel"),
        name="nsa_win_attn",
    )(z, z, z)


def _nsa_combine_kernel(oc_ref, os_ref, ow_ref, zg_ref, ex_ref, nrm_ref, o_ref, mix_s):
    tm = oc_ref.shape[0]
    gates = jax.nn.sigmoid(zg_ref[...])
    g_hi = gates.astype(BF16)
    g_lo = (gates - g_hi.astype(F32)).astype(BF16)
    mix = None
    for br, src in enumerate((oc_ref, os_ref, ow_ref)):
        spread = (jnp.dot(g_hi, ex_ref[br], preferred_element_type=F32)
                  + jnp.dot(g_lo, ex_ref[br], preferred_element_type=F32))
        term = spread * src[...].astype(F32)
        mix = term if mix is None else mix + term
    mix_s[...] = mix
    inv = lax.rsqrt(jnp.mean(mix * mix, axis=-1, keepdims=True) + NORM_EPS)
    o_ref[...] = (mix_s[...] * inv * nrm_ref[...]).astype(o_ref.dtype)


def _gate_spread():
    col = jnp.arange(LANES)[None, :, None]
    head = (jnp.arange(NSA_W) // HEAD_DIM)[None, None, :]
    br = jnp.arange(3)[:, None, None]
    return (col == ZG_NG + 3 * head + br).astype(BF16)


def nsa_combine(o_cmp, o_slc, o_win, zg, norm_g):
    m = o_cmp.shape[0]
    tm = ROW_TM
    ospec = pl.BlockSpec((tm, NSA_W), lambda i: (i, 0))
    return pl.pallas_call(
        _nsa_combine_kernel,
        out_shape=jax.ShapeDtypeStruct((m, NSA_W), BF16),
        grid=(m // tm,),
        in_specs=[ospec, ospec, ospec, pl.BlockSpec((tm, LANES), lambda i: (i, 0)),
                  pl.BlockSpec((3, LANES, NSA_W), lambda i: (0, 0, 0)),
                  pl.BlockSpec((1, NSA_W), lambda i: (0, 0))],
        out_specs=ospec,
        scratch_shapes=[pltpu.VMEM((tm, NSA_W), F32)],
        compiler_params=_cparams("parallel"),
        name="nsa_combine",
    )(o_cmp, o_slc, o_win, zg, _gate_spread(), norm_g.reshape(1, -1))


def _ffn_up_kernel(tiles_per_seq, ntile, x_ref, xp_ref, g_ref, wg0_ref, wgn_ref, wu_ref, cwg_ref, cwu_ref,
                   cbg_ref, cbu_ref, o_ref, h_s, ug_s, uu_s):
    tm = x_ref.shape[0]
    j = pl.program_id(1)

    @pl.when(j == 0)
    def _():
        _fill_halo_h(tiles_per_seq, x_ref, xp_ref, g_ref, h_s)
        ug_s[...] = jnp.dot(h_s[...], wg0_ref[...], preferred_element_type=F32)

    def step(with_next_gate):
        gate = _causal_conv(ug_s, cwg_ref, cbg_ref, FFN_CONV, tm)
        gate = gate * jax.nn.sigmoid(gate)
        uu_s[...] = jnp.dot(h_s[...], wu_ref[...], preferred_element_type=F32)
        if with_next_gate:
            ug_s[...] = jnp.dot(h_s[...], wgn_ref[...], preferred_element_type=F32)
        up = _causal_conv(uu_s, cwu_ref, cbu_ref, FFN_CONV, tm)
        o_ref[...] = (gate * up).astype(o_ref.dtype)

    @pl.when(j < ntile - 1)
    def _():
        step(True)

    @pl.when(j == ntile - 1)
    def _():
        step(False)


def ffn_up(x, g, w_up, layer, conv_w, conv_b, seq):
    m, k = x.shape
    tm, tn, halo = MM_TM, MM_TN, CONV_HALO
    ntile = D_FF // tn
    cb = conv_b.reshape(1, -1)
    next_gate = lambda j: jnp.minimum(j + 1, ntile - 1)
    return pl.pallas_call(
        functools.partial(_ffn_up_kernel, seq // tm, ntile),
        out_shape=jax.ShapeDtypeStruct((m, D_FF), BF16),
        grid=(m // tm, ntile),
        in_specs=[
            pl.BlockSpec((tm, k), lambda i, j: (i, 0)),
            pl.BlockSpec((halo, k), lambda i, j: (jnp.maximum(i * (tm // halo) - 1, 0), 0)),
            pl.BlockSpec((1, k), lambda i, j: (0, 0)),
            pl.BlockSpec((None, k, tn), lambda i, j: (layer, 0, 0)),
            pl.BlockSpec((None, k, tn), lambda i, j: (layer, 0, next_gate(j))),
            pl.BlockSpec((None, k, tn), lambda i, j: (layer, 0, ntile + j)),
            pl.BlockSpec((FFN_CONV, tn), lambda i, j: (0, j)),
            pl.BlockSpec((FFN_CONV, tn), lambda i, j: (0, ntile + j)),
            pl.BlockSpec((1, tn), lambda i, j: (0, j)),
            pl.BlockSpec((1, tn), lambda i, j: (0, ntile + j)),
        ],
        out_specs=pl.BlockSpec((tm, tn), lambda i, j: (i, j)),
        scratch_shapes=[pltpu.VMEM((tm + halo, k), BF16),
                        pltpu.VMEM((tm + halo, tn), F32),
                        pltpu.VMEM((tm + halo, tn), F32)],
        compiler_params=_cparams("parallel", "arbitrary"),
        name="ffn_up",
    )(x, x, g.reshape(1, k), w_up, w_up, w_up, conv_w, conv_w, cb, cb)


def _rmsnorm_kernel(x_ref, g_ref, o_ref):
    o_ref[...] = _rms(x_ref[...], g_ref[...])


def rmsnorm(x, g):
    m, k = x.shape
    tm = ROW_TM
    return pl.pallas_call(
        _rmsnorm_kernel,
        out_shape=jax.ShapeDtypeStruct((m, k), F32),
        grid=(m // tm,),
        in_specs=[pl.BlockSpec((tm, k), lambda i: (i, 0)), pl.BlockSpec((1, k), lambda i: (0, 0))],
        out_specs=pl.BlockSpec((tm, k), lambda i: (i, 0)),
        compiler_params=_cparams("parallel"),
        name="final_rmsnorm",
    )(x, g.reshape(1, k))


def _gate_weight(w_in):
    gates = jnp.concatenate([w_in[..., W_IN_MI:W_IN_MI + 2 * MLSTM_HEADS],
                             w_in[..., W_IN_NG:W_IN_NG + 3 * NSA_HEADS]], axis=-1)
    return jnp.pad(gates, ((0, 0), (0, 0), (0, LANES - gates.shape[-1])))


def _rope_tables(seq):
    pos = jnp.arange(seq, dtype=F32)
    inv = ROPE_THETA ** (-jnp.arange(0, ROPE_DIM, 2, dtype=F32) / ROPE_DIM)
    ang = pos[:, None] * inv[None, :]
    cos, sin = jnp.cos(ang), jnp.sin(ang)
    rest = HEAD_DIM - ROPE_DIM
    cosf = jnp.concatenate([cos, cos, jnp.ones((seq, rest), F32)], axis=1)
    sinf = jnp.concatenate([-sin, sin, jnp.zeros((seq, rest), F32)], axis=1)
    return cosf, sinf


def _overlap_matrix(seq, ncmp_pad):
    nsel = seq // SEL_BLOCK
    cstart = jnp.arange(ncmp_pad) * CMP_STRIDE
    jb = jnp.arange(LANES)
    ov = ((cstart[:, None] < (jb[None, :] + 1) * SEL_BLOCK)
          & (cstart[:, None] + CMP_BLOCK > jb[None, :] * SEL_BLOCK) & (jb[None, :] < nsel))
    return ov.astype(F32)


def _block_onehot(seq):
    blk = jnp.arange(seq)[:, None] // SEL_BLOCK
    return (blk == jnp.arange(LANES)[None, :]).astype(BF16)


def _layer(x, p, wts, layer, cosf, sinf, overlap, onehot, batch, seq):
    z, zg, grp = proj_in(x, p["attn_norm"], wts["w_in"], wts["w_nsa"], wts["w_gate"], layer,
                         p["qk_conv_w"], p["qk_conv_b"], cosf, sinf, seq)

    nchunks = seq // MLSTM_CHUNK
    gate_rows = lambda c0: (zg[:, c0:c0 + MLSTM_HEADS].reshape(batch, seq, MLSTM_HEADS)
                            .transpose(0, 2, 1).reshape(batch, MLSTM_HEADS, nchunks, MLSTM_CHUNK))
    lanes = lambda v: jnp.broadcast_to(v.reshape(MLSTM_HEADS, 1, 1), (MLSTM_HEADS, 1, MLSTM_CHUNK))
    mix_a = mlstm_group(z, gate_rows(ZG_MI), gate_rows(ZG_MF), lanes(p["i_bias"]),
                        lanes(p["f_bias"]), p["mlstm_norm"], batch, seq)

    kcmp = compress(grp, 0, p["cmp_pe_k"], wts["cmp_w1_k"], wts["cmp_w2_k"], layer, batch, seq)
    vcmp = compress(grp, NSA_KV_HEADS, p["cmp_pe_v"], wts["cmp_w1_v"], wts["cmp_w2_v"], layer, batch, seq)
    o_cmp, sel = cmp_select(z, kcmp, vcmp, overlap, batch, seq)
    o_slc = sel_attn(z, onehot, sel, batch, seq)
    o_win = win_attn(z, batch, seq)
    mix_b = nsa_combine(o_cmp, o_slc, o_win, zg, p["nsa_norm"])

    x = matmul_res([mix_a, mix_b], wts["w_out"], layer, x, OUT_TM, D_MODEL)

    act = ffn_up(x, p["ffn_norm"], wts["w_up"], layer, p["ffn_conv_w"], p["ffn_conv_b"], seq)
    x = matmul_res([act], wts["w_down"], layer, x, MM_TM, DOWN_TN)
    return x


_LAYER_PARAMS = ("attn_norm", "w_in", "qk_conv_w", "qk_conv_b", "i_bias", "f_bias", "mlstm_norm",
                 "cmp_pe_k", "cmp_pe_v", "cmp_w1_k", "cmp_w2_k", "cmp_w1_v", "cmp_w2_v", "nsa_norm",
                 "w_out", "ffn_norm", "w_up", "ffn_conv_w", "ffn_conv_b", "w_down")


def kernel(x, attn_norm, w_in, qk_conv_w, qk_conv_b, i_bias, f_bias, mlstm_norm, cmp_pe_k, cmp_pe_v,
           cmp_w1_k, cmp_w2_k, cmp_w1_v, cmp_w2_v, nsa_norm, w_out, ffn_norm, w_up, ffn_conv_w,
           ffn_conv_b, w_down, final_norm):
    stacked = dict(zip(_LAYER_PARAMS, (attn_norm, w_in, qk_conv_w, qk_conv_b, i_bias, f_bias,
                                       mlstm_norm, cmp_pe_k, cmp_pe_v, cmp_w1_k, cmp_w2_k, cmp_w1_v,
                                       cmp_w2_v, nsa_norm, w_out, ffn_norm, w_up, ffn_conv_w,
                                       ffn_conv_b, w_down)))
    batch, seq, d = x.shape
    assert d == D_MODEL and seq % MM_TM == 0 and seq % ATTN_TQ == 0 and seq >= WIN_KEYS
    assert seq // CMP_STRIDE == LANES and seq // SEL_BLOCK <= LANES
    cosf, sinf = _rope_tables(seq)
    overlap = _overlap_matrix(seq, seq // CMP_STRIDE)
    onehot = _block_onehot(seq)
    big = ("w_in", "w_out", "w_up", "w_down", "cmp_w1_k", "cmp_w2_k", "cmp_w1_v", "cmp_w2_v")
    wts = {k: stacked[k].astype(BF16) for k in big}
    wts["w_nsa"] = wts["w_in"][..., W_IN_NSA[0]:W_IN_NSA[1]]
    wts["w_gate"] = _gate_weight(wts["w_in"])
    xf = x.reshape(batch * seq, d)
    for layer in range(DEPTH):
        p = {k: v[layer] for k, v in stacked.items() if k not in big}
        xf = _layer(xf, p, wts, layer, cosf, sinf, overlap, onehot, batch, seq)
    return rmsnorm(xf, final_norm).reshape(batch, seq, d)
```

```python
import functools

import jax
import jax.numpy as jnp
from jax import lax
from jax.experimental import pallas as pl
from jax.experimental.pallas import tpu as pltpu

D_MODEL = 2048
DEPTH = 2
MLSTM_HEADS = 8
HEAD_DIM = 128
MLSTM_W = MLSTM_HEADS * HEAD_DIM
QK_CONV = 4
NSA_HEADS = 8
NSA_KV_HEADS = 2
NSA_GROUP = NSA_HEADS // NSA_KV_HEADS
NSA_W = NSA_HEADS * HEAD_DIM
NSA_KV_W = NSA_KV_HEADS * HEAD_DIM
CMP_BLOCK = 32
CMP_STRIDE = 16
CMP_HIDDEN = 256
SEL_BLOCK = 64
SEL_TOPK = 16
WINDOW = 512
ROPE_THETA = 500000.0
ROPE_DIM = HEAD_DIM // 4
D_FF = 5632
FFN_CONV = 3
NORM_EPS = 1e-6
NEG = -1e30
FORCE_SCORE = 1e4
ATTN_SCALE = HEAD_DIM ** -0.5

LANES = 128
SUBLANES_F32 = 8
SUBLANES_BF16 = 16
VMEM_LIMIT_BYTES = 48 * 1024 * 1024

MM_TM = 1024
MM_TN = 512
EPI_ROWS = 128
DOWN_TN = 512
OUT_TM = 512
MLSTM_CHUNK = 256
MLSTM_HEADS_PER_STEP = 4
ATTN_TQ = 256
CMP_TQ = 512
ATTN_TK = 256
WIN_KEYS = WINDOW + ATTN_TQ
ROW_TM = 512
CONV_HALO = SUBLANES_BF16

BF16 = jnp.bfloat16
F32 = jnp.float32

W_IN_MI = 4 * MLSTM_W
W_IN_NSA = (W_IN_MI + 2 * MLSTM_HEADS, W_IN_MI + 2 * MLSTM_HEADS + NSA_W + 6 * NSA_KV_W)
W_IN_NG = W_IN_NSA[1]
Z_NQ = 0
Z_KC, Z_VC, Z_KS, Z_VS, Z_KW, Z_VW = (Z_NQ + NSA_W + i * NSA_KV_W for i in range(6))
ZN_W = Z_VW + NSA_KV_W
ZG_MI, ZG_MF, ZG_NG = 0, MLSTM_HEADS, 2 * MLSTM_HEADS


def _cparams(*sem):
    return pltpu.CompilerParams(dimension_semantics=sem, vmem_limit_bytes=VMEM_LIMIT_BYTES)


def _rms(x, gain):
    return x * lax.rsqrt(jnp.mean(x * x, axis=-1, keepdims=True) + NORM_EPS) * gain


def _nt_dot(a, b):
    return lax.dot_general(a, b, (((1,), (1,)), ((), ())), preferred_element_type=F32)


def _stack_heads(q_ref):
    return jnp.concatenate([q_ref[:, g * HEAD_DIM:(g + 1) * HEAD_DIM] for g in range(NSA_GROUP)], axis=0)


def _fill_halo_h(tiles_per_seq, x_ref, xp_ref, g_ref, h_s):
    tm = x_ref.shape[0]
    first = pl.program_id(0) % tiles_per_seq == 0
    hp = _rms(xp_ref[...], g_ref[...])
    h_s[0:CONV_HALO, :] = jnp.where(first, 0.0, hp).astype(h_s.dtype)
    h_s[CONV_HALO:CONV_HALO + tm, :] = _rms(x_ref[...], g_ref[...]).astype(h_s.dtype)


def _causal_conv(u_s, cw_ref, cb_ref, taps, tm):
    u = u_s[...]
    acc = jnp.broadcast_to(cb_ref[...], (tm, u_s.shape[1]))
    for j in range(taps):
        back = taps - 1 - j
        shifted = pltpu.roll(u, back, axis=0) if back else u
        acc = acc + shifted[CONV_HALO:CONV_HALO + tm, :] * cw_ref[j:j + 1, :]
    return acc


def _rope(x, cosf, sinf, lane):
    rot = jnp.where(lane < ROPE_DIM // 2, pltpu.roll(x, LANES - ROPE_DIM // 2, axis=1),
                    pltpu.roll(x, ROPE_DIM // 2, axis=1))
    return x * cosf + rot * sinf


def _proj_in_kernel(tiles_per_seq, n_pair, x_ref, xp_ref, g_ref, wc_ref, wp_ref, wn_ref, wg_ref, cw_ref, cb_ref,
                    cos_ref, sin_ref, oc_ref, op_ref, o_ref, zg_ref, grp_ref, h_s, u_s, y_s):
    tm, tn = o_ref.shape
    j = pl.program_id(1)
    j_k = MLSTM_W // tn
    j_nkv = n_pair + Z_KC // tn

    @pl.when(j == 0)
    def _():
        _fill_halo_h(tiles_per_seq, x_ref, xp_ref, g_ref, h_s)
        zg_ref[...] = jnp.dot(h_s[CONV_HALO:CONV_HALO + tm, :], wg_ref[...], preferred_element_type=F32)
        grp_ref[...] = jnp.zeros_like(grp_ref)
        o_ref[...] = jnp.zeros_like(o_ref)

    @pl.when(j < n_pair)
    def _():
        u_s[...] = jnp.dot(h_s[...], wc_ref[...], preferred_element_type=F32)
        op_ref[...] = jnp.dot(h_s[CONV_HALO:CONV_HALO + tm, :], wp_ref[...],
                              preferred_element_type=F32).astype(op_ref.dtype)
        y = _causal_conv(u_s, cw_ref, cb_ref, QK_CONV, tm)
        scale = jnp.where(j >= j_k, ATTN_SCALE, 1.0)
        oc_ref[...] = (y * jax.nn.sigmoid(y) * scale).astype(oc_ref.dtype)

    @pl.when(j >= n_pair)
    def _():
        is_q = j < j_nkv
        for r in range(tm // EPI_ROWS):
            rs = slice(r * EPI_ROWS, (r + 1) * EPI_ROWS)
            cosf, sinf = cos_ref[rs, :], sin_ref[rs, :]
            lane = lax.broadcasted_iota(jnp.int32, cosf.shape, 1)
            u = jnp.dot(h_s[CONV_HALO + r * EPI_ROWS:CONV_HALO + (r + 1) * EPI_ROWS, :], wn_ref[...],
                        preferred_element_type=F32)
            for head in range(tn // HEAD_DIM):
                cs = slice(head * HEAD_DIM, (head + 1) * HEAD_DIM)
                xh = u[:, cs]
                roped = _rope(xh, cosf, sinf, lane)
                if head < NSA_KV_HEADS:
                    yh = roped * jnp.where(is_q, ATTN_SCALE, 1.0)
                else:
                    yh = jnp.where(is_q, roped * ATTN_SCALE, xh)
                o_ref[rs, cs] = yh.astype(o_ref.dtype)
                y_s[head, rs, :] = yh

        @pl.when(j == j_nkv)
        def _():
            ngrp = tm // CMP_STRIDE
            for head in range(tn // HEAD_DIM):
                for l in range(CMP_STRIDE):
                    col = (head * CMP_STRIDE + l) * HEAD_DIM
                    grp_ref[:, col:col + HEAD_DIM] = y_s[head, pl.ds(l, ngrp, stride=CMP_STRIDE),
                                                         :].astype(grp_ref.dtype)


def proj_in(x, g, w_in, w_nsa, w_gate, layer, conv_w, conv_b, cosf, sinf, seq):
    m, k = x.shape
    tm, tn, halo = MM_TM, MM_TN, CONV_HALO
    assert tn == 2 * NSA_KV_W and MLSTM_W % tn == 0 and Z_KC % tn == 0
    n_pair = 2 * MLSTM_W // tn
    nper = seq // tm
    pair = lambda j: jnp.minimum(j, n_pair - 1)
    nsa = lambda j: jnp.maximum(j - n_pair, 0)
    wide = jax.ShapeDtypeStruct((m, 2 * MLSTM_W), BF16)
    return pl.pallas_call(
        functools.partial(_proj_in_kernel, seq // tm, n_pair),
        out_shape=(wide, wide, jax.ShapeDtypeStruct((m, ZN_W), BF16), jax.ShapeDtypeStruct((m, LANES), F32),
                   jax.ShapeDtypeStruct((m // CMP_STRIDE, tn * CMP_STRIDE), BF16)),
        grid=(m // tm, n_pair + ZN_W // tn),
        in_specs=[
            pl.BlockSpec((tm, k), lambda i, j: (i, 0)),
            pl.BlockSpec((halo, k), lambda i, j: (jnp.maximum(i * (tm // halo) - 1, 0), 0)),
            pl.BlockSpec((1, k), lambda i, j: (0, 0)),
            pl.BlockSpec((None, k, tn), lambda i, j: (layer, 0, pair(j))),
            pl.BlockSpec((None, k, tn), lambda i, j: (layer, 0, n_pair + pair(j))),
            pl.BlockSpec((None, k, tn), lambda i, j: (layer, 0, nsa(j))),
            pl.BlockSpec((None, k, LANES), lambda i, j: (layer, 0, 0)),
            pl.BlockSpec((QK_CONV, tn), lambda i, j: (0, pair(j))),
            pl.BlockSpec((1, tn), lambda i, j: (0, pair(j))),
            pl.BlockSpec((tm, LANES), lambda i, j: (i % nper, 0)),
            pl.BlockSpec((tm, LANES), lambda i, j: (i % nper, 0)),
        ],
        out_specs=(pl.BlockSpec((tm, tn), lambda i, j: (i, pair(j))),
                   pl.BlockSpec((tm, tn), lambda i, j: (i, pair(j))),
                   pl.BlockSpec((tm, tn), lambda i, j: (i, nsa(j))),
                   pl.BlockSpec((tm, LANES), lambda i, j: (i, 0)),
                   pl.BlockSpec((tm // CMP_STRIDE, tn * CMP_STRIDE), lambda i, j: (i, 0))),
        scratch_shapes=[pltpu.VMEM((tm + halo, k), BF16), pltpu.VMEM((tm + halo, tn), F32),
                        pltpu.VMEM((tn // HEAD_DIM, tm, HEAD_DIM), F32)],
        compiler_params=_cparams("parallel", "arbitrary"),
        name="proj_in",
    )(x, x, g.reshape(1, k), w_in, w_in, w_nsa, w_gate, conv_w, conv_b.reshape(1, -1), cosf, sinf)


def _matmul_res_kernel(n_lhs, *refs):
    lhs = refs[:n_lhs]
    ws = refs[n_lhs:2 * n_lhs]
    res_ref, o_ref = refs[2 * n_lhs], refs[2 * n_lhs + 1]
    acc = res_ref[...]
    for a_ref, w_ref in zip(lhs, ws):
        acc = acc + jnp.dot(a_ref[...], w_ref[...], preferred_element_type=F32)
    o_ref[...] = acc


def matmul_res(lhs_list, w, layer, res, tm, tn):
    m, n = res.shape
    n_lhs = len(lhs_list)
    kb = w.shape[1] // n_lhs
    assert all(a.shape[1] == kb for a in lhs_list)
    in_specs = [pl.BlockSpec((tm, kb), lambda i, j: (i, 0)) for _ in lhs_list]
    in_specs += [pl.BlockSpec((None, kb, tn), lambda i, j, p=p: (layer, p, j)) for p in range(n_lhs)]
    in_specs += [pl.BlockSpec((tm, tn), lambda i, j: (i, j))]
    return pl.pallas_call(
        functools.partial(_matmul_res_kernel, n_lhs),
        out_shape=jax.ShapeDtypeStruct((m, n), F32),
        grid=(m // tm, n // tn),
        in_specs=in_specs,
        out_specs=pl.BlockSpec((tm, tn), lambda i, j: (i, j)),
        compiler_params=_cparams("parallel", "arbitrary"),
        name="matmul_res",
    )(*lhs_list, *([w] * n_lhs), res)


def _mlstm_kernel(seq, q_ref, k_ref, v_ref, zo_ref, gi_ref, gf_ref, bi_ref, bf_ref, nrm_ref, o_ref,
                  b_s, i_s, cx_s, m_s):
    L = MLSTM_CHUNK
    hps = MLSTM_HEADS_PER_STEP
    nchunks = seq // L

    lane = lax.broadcasted_iota(jnp.int32, (nchunks, L), 1)
    for hh in range(hps):
        i_s[hh] = gi_ref[0, hh] + bi_ref[hh]
        fpre = gf_ref[0, hh] + bf_ref[hh]
        csum = jnp.minimum(fpre, 0.0) - jnp.log1p(jnp.exp(-jnp.abs(fpre)))
        for sh in (1 << e for e in range(L.bit_length() - 1)):
            csum = csum + jnp.where(lane >= sh, pltpu.roll(csum, sh, axis=1), 0.0)
        b_s[hh] = csum

    cx_s[...] = jnp.zeros_like(cx_s)
    m_s[...] = jnp.zeros_like(m_s)

    row = lax.broadcasted_iota(jnp.int32, (L, L), 0)
    col = lax.broadcasted_iota(jnp.int32, (L, L), 1)
    eye = row == col
    causal = col <= row
    ones = jnp.ones((L, LANES), BF16)

    def chunk_body(c, carry):
        r0 = pl.multiple_of(c * L, L)
        for hh in range(hps):
            cs = slice(hh * HEAD_DIM, (hh + 1) * HEAD_DIM)
            qc = q_ref[pl.ds(r0, L), cs]
            kc = k_ref[pl.ds(r0, L), cs]
            vx = jnp.concatenate([v_ref[pl.ds(r0, L), cs], ones], axis=1)
            b_row = b_s[hh, pl.ds(c, 1), :]
            u_row = i_s[hh, pl.ds(c, 1), :] - b_row
            b_col = jnp.sum(jnp.where(eye, b_row, 0.0), axis=-1, keepdims=True)
            m_prev = m_s[hh]
            cx = cx_s[hh]

            umat = jnp.where(causal, u_row, -jnp.inf)
            m_col = jnp.maximum(m_prev, jnp.max(umat, axis=-1, keepdims=True))
            w_inter = jnp.exp(m_prev - m_col)
            s_qk = _nt_dot(qc, kc) * jnp.exp(umat - m_col)
            nd = (w_inter * jnp.dot(qc, cx.astype(BF16), preferred_element_type=F32)
                  + jnp.dot(s_qk.astype(BF16), vx, preferred_element_type=F32))
            floor = jnp.exp(-(b_col + m_col))
            h = nd[:, :HEAD_DIM] / jnp.maximum(jnp.abs(nd[:, HEAD_DIM:]), floor)

            m_last = m_col[L - 1:L, :]
            kw_t = kc.astype(F32).T * jnp.exp(u_row - m_last)
            cx_s[hh] = (jnp.exp(m_prev - m_last) * cx
                        + jnp.dot(kw_t.astype(BF16), vx, preferred_element_type=F32))
            m_s[hh] = b_row[:, L - 1:L] + m_last

            hn = _rms(h, nrm_ref[:, cs])
            gate = jax.nn.sigmoid(zo_ref[pl.ds(r0, L), cs].astype(F32))
            o_ref[pl.ds(r0, L), cs] = (hn * gate).astype(o_ref.dtype)
        return carry

    lax.fori_loop(0, nchunks, chunk_body, 0)


def mlstm_group(zc, zp, gi, gf, bi, bf, norm_g, batch, seq):
    hps = MLSTM_HEADS_PER_STEP
    wblk = hps * HEAD_DIM
    nchunks = seq // MLSTM_CHUNK
    nhb = MLSTM_HEADS // hps
    lo = pl.BlockSpec((seq, wblk), lambda b, h: (b, h))
    hi = pl.BlockSpec((seq, wblk), lambda b, h: (b, nhb + h))
    gspec = pl.BlockSpec((1, hps, nchunks, MLSTM_CHUNK), lambda b, h: (b, h, 0, 0))
    bspec = pl.BlockSpec((hps, 1, MLSTM_CHUNK), lambda b, h: (h, 0, 0))
    return pl.pallas_call(
        functools.partial(_mlstm_kernel, seq),
        out_shape=jax.ShapeDtypeStruct((batch * seq, MLSTM_W), BF16),
        grid=(batch, nhb),
        in_specs=[lo, hi, lo, hi, gspec, gspec, bspec, bspec,
                  pl.BlockSpec((1, wblk), lambda b, h: (0, h))],
        out_specs=pl.BlockSpec((seq, wblk), lambda b, h: (b, h)),
        scratch_shapes=[
            pltpu.VMEM((hps, nchunks, MLSTM_CHUNK), F32),
            pltpu.VMEM((hps, nchunks, MLSTM_CHUNK), F32),
            pltpu.VMEM((hps, HEAD_DIM, 2 * HEAD_DIM), F32),
            pltpu.VMEM((hps, 1, 1), F32),
        ],
        compiler_params=_cparams("parallel", "parallel"),
        name="mlstm_group",
    )(zc, zc, zp, zp, gi, gf, bi, bf, norm_g.reshape(1, -1))


def _compress_kernel(x_ref, pe_ref, w1_ref, w2_ref, o_ref, sh_s):
    half = CMP_STRIDE * HEAD_DIM
    ngroups = x_ref.shape[0]
    x = x_ref[...].astype(F32)
    lo = jnp.dot((x + pe_ref[:, :half]).astype(BF16), w1_ref[:half, :], preferred_element_type=F32)
    hi = jnp.dot((x + pe_ref[:, half:]).astype(BF16), w1_ref[half:, :], preferred_element_type=F32)
    sh_s[0:ngroups, :] = hi
    sh_s[ngroups:ngroups + SUBLANES_F32, :] = jnp.zeros((SUBLANES_F32, CMP_HIDDEN), F32)
    pre = lo + sh_s[1:ngroups + 1, :]
    act = jax.nn.gelu(pre, approximate=True)
    o_ref[0] = jnp.dot(act.astype(BF16), w2_ref[...], preferred_element_type=F32).astype(o_ref.dtype)


def compress(grp, first_head, pe, w1, w2, layer, batch, seq):
    nslab, ngroups, gw = batch * NSA_KV_HEADS, seq // CMP_STRIDE, CMP_STRIDE * HEAD_DIM
    return pl.pallas_call(
        _compress_kernel,
        out_shape=jax.ShapeDtypeStruct((nslab, ngroups, HEAD_DIM), BF16),
        grid=(nslab,),
        in_specs=[pl.BlockSpec((ngroups, gw), lambda s: (s // NSA_KV_HEADS, first_head + s % NSA_KV_HEADS)),
                  pl.BlockSpec((1, 2 * gw), lambda s: (0, 0)),
                  pl.BlockSpec((None, 2 * gw, CMP_HIDDEN), lambda s: (layer, 0, 0)),
                  pl.BlockSpec((None, CMP_HIDDEN, HEAD_DIM), lambda s: (layer, 0, 0))],
        out_specs=pl.BlockSpec((1, ngroups, HEAD_DIM), lambda s: (s, 0, 0)),
        scratch_shapes=[pltpu.VMEM((ngroups + SUBLANES_F32, CMP_HIDDEN), F32)],
        compiler_params=_cparams("parallel"),
        name="nsa_compress",
    )(grp, pe.reshape(1, -1), w1, w2)


def _cmp_select_kernel(nsel, q_ref, kc_ref, vc_ref, ov_ref, o_ref, sel_ref):
    tq = q_ref.shape[0]
    s0 = pl.program_id(2) * tq
    kc = kc_ref[0]
    vc = vc_ref[0]
    ncp = kc.shape[0]
    pos = s0 + lax.broadcasted_iota(jnp.int32, (tq, ncp), 0)
    cidx = lax.broadcasted_iota(jnp.int32, (tq, ncp), 1)
    valid = cidx * CMP_STRIDE + (CMP_BLOCK - 1) <= pos
    any_valid = s0 + lax.broadcasted_iota(jnp.int32, (tq, 1), 0) >= CMP_BLOCK - 1
    sc = _nt_dot(_stack_heads(q_ref), kc)
    sc = jnp.where(valid[None], sc.reshape(NSA_GROUP, tq, ncp), NEG)
    e = jnp.where(valid[None], jnp.exp(sc - jnp.max(sc, axis=-1, keepdims=True)), 0.0)
    denom = jnp.sum(e, axis=-1, keepdims=True)
    p = e / jnp.where(any_valid[None], denom, 1.0)
    o = jnp.dot(p.reshape(NSA_GROUP * tq, ncp).astype(BF16), vc, preferred_element_type=F32)
    p_sum = p[0]
    for g in range(NSA_GROUP):
        o_ref[:, g * HEAD_DIM:(g + 1) * HEAD_DIM] = o[g * tq:(g + 1) * tq, :].astype(o_ref.dtype)
        if g:
            p_sum = p_sum + p[g]
    imp = jnp.dot(p_sum, ov_ref[...], preferred_element_type=F32, precision=lax.Precision.HIGHEST)
    jb = cidx
    cur = pos // SEL_BLOCK
    forced = (jb == 0) | (jb == cur) | (jb == cur - 1)
    imp = jnp.where(forced, FORCE_SCORE, imp)
    imp = jnp.where(jb <= cur, imp, -1.0)
    imp_t = imp.T[:nsel, :]
    jb_t = lax.broadcasted_iota(jnp.int32, (nsel, tq), 0)
    rank = jnp.zeros((nsel, tq), F32)
    for j2 in range(nsel):
        rowv = imp_t[j2:j2 + 1, :]
        before = (rowv > imp_t) | ((rowv == imp_t) & (jb_t > j2))
        rank = rank + jnp.where(before, 1.0, 0.0)
    sel_t = jnp.where(rank < float(SEL_TOPK), 1.0, 0.0)
    sel_t = jnp.concatenate([sel_t, jnp.zeros((ncp - nsel, tq), F32)], axis=0)
    sel_ref[...] = sel_t.T.astype(sel_ref.dtype)


def cmp_select(z, kcmp, vcmp, overlap, batch, seq):
    m = z.shape[0]
    tq = CMP_TQ
    nq = seq // tq
    ncp = kcmp.shape[1]
    gw = NSA_GROUP * HEAD_DIM
    return pl.pallas_call(
        functools.partial(_cmp_select_kernel, seq // SEL_BLOCK),
        out_shape=(jax.ShapeDtypeStruct((m, NSA_W), BF16),
                   jax.ShapeDtypeStruct((m, NSA_KV_HEADS * LANES), BF16)),
        grid=(batch, NSA_KV_HEADS, nq),
        in_specs=[pl.BlockSpec((tq, gw), lambda b, h, i: (b * nq + i, Z_NQ // gw + h)),
                  pl.BlockSpec((1, ncp, HEAD_DIM), lambda b, h, i: (b * NSA_KV_HEADS + h, 0, 0)),
                  pl.BlockSpec((1, ncp, HEAD_DIM), lambda b, h, i: (b * NSA_KV_HEADS + h, 0, 0)),
                  pl.BlockSpec((ncp, LANES), lambda b, h, i: (0, 0))],
        out_specs=(pl.BlockSpec((tq, gw), lambda b, h, i: (b * nq + i, h)),
                   pl.BlockSpec((tq, LANES), lambda b, h, i: (b * nq + i, h))),
        compiler_params=_cparams("parallel", "parallel", "parallel"),
        name="nsa_cmp_select",
    )(z, kcmp, vcmp, overlap)


def _sel_attn_kernel(q_ref, k_ref, v_ref, oh_ref, sel_ref, o_ref, qx_s, m_s, acc_s, s0_s, s1_s):
    tq, tk = ATTN_TQ, ATTN_TK
    i = pl.program_id(2)
    penalty = ((sel_ref[...].astype(F32) - 1.0) * (-NEG)).astype(BF16)
    for g in range(NSA_GROUP):
        cs = slice(g * HEAD_DIM, (g + 1) * HEAD_DIM)
        qx_s[g * tq:(g + 1) * tq, :] = jnp.concatenate([q_ref[:, cs], penalty], axis=1)
    m_s[...] = jnp.full_like(m_s, NEG)
    acc_s[...] = jnp.zeros_like(acc_s)
    ones = jnp.ones((tk, LANES), BF16)

    def scores(t, s_buf):
        k0 = pl.multiple_of(t * tk, tk)
        kx = jnp.concatenate([k_ref[pl.ds(k0, tk), :], oh_ref[pl.ds(k0, tk), :]], axis=1)
        s_buf[...] = _nt_dot(qx_s[...], kx)

    def update(t, s_buf, diagonal):
        k0 = pl.multiple_of(t * tk, tk)
        vx = jnp.concatenate([v_ref[pl.ds(k0, tk), :], ones], axis=1)
        s = s_buf[...]
        if diagonal:
            row = lax.broadcasted_iota(jnp.int32, (tq, tk), 0)
            col = lax.broadcasted_iota(jnp.int32, (tq, tk), 1)
            causal = jnp.where(col <= row, 0.0, NEG)
            s = (s.reshape(NSA_GROUP, tq, tk) + causal[None]).reshape(NSA_GROUP * tq, tk)
        m_old = m_s[...]
        m_new = jnp.maximum(m_old, jnp.max(s, axis=-1, keepdims=True))
        alpha = jnp.exp(m_old - m_new)
        p = jnp.exp((s - jnp.concatenate([m_new] * (tk // LANES), axis=1)).astype(BF16))
        acc_s[...] = (jnp.concatenate([alpha, alpha], axis=1) * acc_s[...]
                      + jnp.dot(p, vx, preferred_element_type=F32))
        m_s[...] = m_new

    scores(0, s0_s)

    def pair(p, carry):
        t = 2 * p
        scores(t + 1, s1_s)
        update(t, s0_s, False)
        scores(t + 2, s0_s)
        update(t + 1, s1_s, False)
        return carry

    lax.fori_loop(0, i // 2, pair, 0)

    @pl.when(i % 2 == 0)
    def _():
        update(i, s0_s, True)

    @pl.when(i % 2 == 1)
    def _():
        scores(i, s1_s)
        update(i - 1, s0_s, False)
        update(i, s1_s, True)
    for g in range(NSA_GROUP):
        cs = slice(g * HEAD_DIM, (g + 1) * HEAD_DIM)
        acc = acc_s[g * tq:(g + 1) * tq, :]
        o_ref[:, cs] = (acc[:, :HEAD_DIM] / acc[:, HEAD_DIM:]).astype(o_ref.dtype)


def _kv_spec(seq, col0):
    return pl.BlockSpec((seq, HEAD_DIM), lambda b, h, i: (b, col0 // HEAD_DIM + h))


def sel_attn(z, onehot, sel, batch, seq):
    assert ATTN_TQ == ATTN_TK
    m = z.shape[0]
    tq = ATTN_TQ
    nq = seq // tq
    gw = NSA_GROUP * HEAD_DIM
    return pl.pallas_call(
        _sel_attn_kernel,
        out_shape=jax.ShapeDtypeStruct((m, NSA_W), BF16),
        grid=(batch, NSA_KV_HEADS, nq),
        in_specs=[pl.BlockSpec((tq, gw), lambda b, h, i: (b * nq + i, Z_NQ // gw + h)),
                  _kv_spec(seq, Z_KS), _kv_spec(seq, Z_VS),
                  pl.BlockSpec((seq, LANES), lambda b, h, i: (0, 0)),
                  pl.BlockSpec((tq, LANES), lambda b, h, i: (b * nq + i, h))],
        out_specs=pl.BlockSpec((tq, gw), lambda b, h, i: (b * nq + i, h)),
        scratch_shapes=[pltpu.VMEM((NSA_GROUP * tq, 2 * LANES), BF16),
                        pltpu.VMEM((NSA_GROUP * tq, LANES), F32),
                        pltpu.VMEM((NSA_GROUP * tq, 2 * LANES), F32),
                        pltpu.VMEM((NSA_GROUP * tq, ATTN_TK), F32),
                        pltpu.VMEM((NSA_GROUP * tq, ATTN_TK), F32)],
        compiler_params=_cparams("parallel", "parallel", "arbitrary"),
        name="nsa_sel_attn",
    )(z, z, z, onehot, sel)


def _win_attn_kernel(q_ref, k_ref, v_ref, o_ref):
    tq, nk = ATTN_TQ, WIN_KEYS
    s0 = pl.program_id(2) * tq
    start = pl.multiple_of(jnp.maximum(s0 - WINDOW, 0), tq)
    kw = k_ref[pl.ds(start, nk), :]
    vx = jnp.concatenate([v_ref[pl.ds(start, nk), :], jnp.ones((nk, LANES), BF16)], axis=1)
    qpos = s0 + lax.broadcasted_iota(jnp.int32, (tq, nk), 0)
    kpos = start + lax.broadcasted_iota(jnp.int32, (tq, nk), 1)
    diff = qpos - kpos
    bias = jnp.where((diff >= 0) & (diff < WINDOW), 0.0, NEG)
    for g in range(NSA_GROUP):
        cs = slice(g * HEAD_DIM, (g + 1) * HEAD_DIM)
        s = _nt_dot(q_ref[:, cs], kw) + bias
        p = jnp.exp((s - jnp.max(s, axis=-1, keepdims=True)).astype(BF16))
        a = jnp.dot(p, vx, preferred_element_type=F32)
        o_ref[:, cs] = (a[:, :HEAD_DIM] / a[:, HEAD_DIM:]).astype(o_ref.dtype)


def win_attn(z, batch, seq):
    m = z.shape[0]
    tq = ATTN_TQ
    nq = seq // tq
    gw = NSA_GROUP * HEAD_DIM
    return pl.pallas_call(
        _win_attn_kernel,
        out_shape=jax.ShapeDtypeStruct((m, NSA_W), BF16),
        grid=(batch, NSA_KV_HEADS, nq),
        in_specs=[pl.BlockSpec((tq, gw), lambda b, h, i: (b * nq + i, Z_NQ // gw + h)),
                  _kv_spec(seq, Z_KW), _kv_spec(seq, Z_VW)],
        out_specs=pl.BlockSpec((tq, gw), lambda b, h, i: (b * nq + i, h)),
        compiler_params=_cparams("parallel", "parallel", "parallel"),
        name="nsa_win_attn",
    )(z, z, z)


def _nsa_combine_kernel(oc_ref, os_ref, ow_ref, zg_ref, ex_ref, nrm_ref, o_ref, mix_s):
    tm = oc_ref.shape[0]
    gates = jax.nn.sigmoid(zg_ref[...])
    g_hi = gates.astype(BF16)
    g_lo = (gates - g_hi.astype(F32)).astype(BF16)
    mix = None
    for br, src in enumerate((oc_ref, os_ref, ow_ref)):
        spread = (jnp.dot(g_hi, ex_ref[br], preferred_element_type=F32)
                  + jnp.dot(g_lo, ex_ref[br], preferred_element_type=F32))
        term = spread * src[...].astype(F32)
        mix = term if mix is None else mix + term
    mix_s[...] = mix
    inv = lax.rsqrt(jnp.mean(mix * mix, axis=-1, keepdims=True) + NORM_EPS)
    o_ref[...] = (mix_s[...] * inv * nrm_ref[...]).astype(o_ref.dtype)


def _gate_spread():
    col = jnp.arange(LANES)[None, :, None]
    head = (jnp.arange(NSA_W) // HEAD_DIM)[None, None, :]
    br = jnp.arange(3)[:, None, None]
    return (col == ZG_NG + 3 * head + br).astype(BF16)


def nsa_combine(o_cmp, o_slc, o_win, zg, norm_g):
    m = o_cmp.shape[0]
    tm = ROW_TM
    ospec = pl.BlockSpec((tm, NSA_W), lambda i: (i, 0))
    return pl.pallas_call(
        _nsa_combine_kernel,
        out_shape=jax.ShapeDtypeStruct((m, NSA_W), BF16),
        grid=(m // tm,),
        in_specs=[ospec, ospec, ospec, pl.BlockSpec((tm, LANES), lambda i: (i, 0)),
                  pl.BlockSpec((3, LANES, NSA_W), lambda i: (0, 0, 0)),
                  pl.BlockSpec((1, NSA_W), lambda i: (0, 0))],
        out_specs=ospec,
        scratch_shapes=[pltpu.VMEM((tm, NSA_W), F32)],
        compiler_params=_cparams("parallel"),
        name="nsa_combine",
    )(o_cmp, o_slc, o_win, zg, _gate_spread(), norm_g.reshape(1, -1))


def _ffn_up_kernel(tiles_per_seq, ntile, x_ref, xp_ref, g_ref, wg0_ref, wgn_ref, wu_ref, cwg_ref, cwu_ref,
                   cbg_ref, cbu_ref, o_ref, h_s, ug_s, uu_s):
    tm = x_ref.shape[0]
    j = pl.program_id(1)

    @pl.when(j == 0)
    def _():
        _fill_halo_h(tiles_per_seq, x_ref, xp_ref, g_ref, h_s)
        ug_s[...] = jnp.dot(h_s[...], wg0_ref[...], preferred_element_type=F32)

    def step(with_next_gate):
        gate = _causal_conv(ug_s, cwg_ref, cbg_ref, FFN_CONV, tm)
        gate = gate * jax.nn.sigmoid(gate)
        uu_s[...] = jnp.dot(h_s[...], wu_ref[...], preferred_element_type=F32)
        if with_next_gate:
            ug_s[...] = jnp.dot(h_s[...], wgn_ref[...], preferred_element_type=F32)
        up = _causal_conv(uu_s, cwu_ref, cbu_ref, FFN_CONV, tm)
        o_ref[...] = (gate * up).astype(o_ref.dtype)

    @pl.when(j < ntile - 1)
    def _():
        step(True)

    @pl.when(j == ntile - 1)
    def _():
        step(False)


def ffn_up(x, g, w_up, layer, conv_w, conv_b, seq):
    m, k = x.shape
    tm, tn, halo = MM_TM, MM_TN, CONV_HALO
    ntile = D_FF // tn
    cb = conv_b.reshape(1, -1)
    next_gate = lambda j: jnp.minimum(j + 1, ntile - 1)
    return pl.pallas_call(
        functools.partial(_ffn_up_kernel, seq // tm, ntile),
        out_shape=jax.ShapeDtypeStruct((m, D_FF), BF16),
        grid=(m // tm, ntile),
        in_specs=[
            pl.BlockSpec((tm, k), lambda i, j: (i, 0)),
            pl.BlockSpec((halo, k), lambda i, j: (jnp.maximum(i * (tm // halo) - 1, 0), 0)),
            pl.BlockSpec((1, k), lambda i, j: (0, 0)),
            pl.BlockSpec((None, k, tn), lambda i, j: (layer, 0, 0)),
            pl.BlockSpec((None, k, tn), lambda i, j: (layer, 0, next_gate(j))),
            pl.BlockSpec((None, k, tn), lambda i, j: (layer, 0, ntile + j)),
            pl.BlockSpec((FFN_CONV, tn), lambda i, j: (0, j)),
            pl.BlockSpec((FFN_CONV, tn), lambda i, j: (0, ntile + j)),
            pl.BlockSpec((1, tn), lambda i, j: (0, j)),
            pl.BlockSpec((1, tn), lambda i, j: (0, ntile + j)),
        ],
        out_specs=pl.BlockSpec((tm, tn), lambda i, j: (i, j)),
        scratch_shapes=[pltpu.VMEM((tm + halo, k), BF16),
                        pltpu.VMEM((tm + halo, tn), F32),
                        pltpu.VMEM((tm + halo, tn), F32)],
        compiler_params=_cparams("parallel", "arbitrary"),
        name="ffn_up",
    )(x, x, g.reshape(1, k), w_up, w_up, w_up, conv_w, conv_w, cb, cb)


def _rmsnorm_kernel(x_ref, g_ref, o_ref):
    o_ref[...] = _rms(x_ref[...], g_ref[...])


def rmsnorm(x, g):
    m, k = x.shape
    tm = ROW_TM
    return pl.pallas_call(
        _rmsnorm_kernel,
        out_shape=jax.ShapeDtypeStruct((m, k), F32),
        grid=(m // tm,),
        in_specs=[pl.BlockSpec((tm, k), lambda i: (i, 0)), pl.BlockSpec((1, k), lambda i: (0, 0))],
        out_specs=pl.BlockSpec((tm, k), lambda i: (i, 0)),
        compiler_params=_cparams("parallel"),
        name="final_rmsnorm",
    )(x, g.reshape(1, k))


def _gate_weight(w_in):
    gates = jnp.concatenate([w_in[..., W_IN_MI:W_IN_MI + 2 * MLSTM_HEADS],
                             w_in[..., W_IN_NG:W_IN_NG + 3 * NSA_HEADS]], axis=-1)
    return jnp.pad(gates, ((0, 0), (0, 0), (0, LANES - gates.shape[-1])))


def _rope_tables(seq):
    pos = jnp.arange(seq, dtype=F32)
    inv = ROPE_THETA ** (-jnp.arange(0, ROPE_DIM, 2, dtype=F32) / ROPE_DIM)
    ang = pos[:, None] * inv[None, :]
    cos, sin = jnp.cos(ang), jnp.sin(ang)
    rest = HEAD_DIM - ROPE_DIM
    cosf = jnp.concatenate([cos, cos, jnp.ones((seq, rest), F32)], axis=1)
    sinf = jnp.concatenate([-sin, sin, jnp.zeros((seq, rest), F32)], axis=1)
    return cosf, sinf


def _overlap_matrix(seq, ncmp_pad):
    nsel = seq // SEL_BLOCK
    cstart = jnp.arange(ncmp_pad) * CMP_STRIDE
    jb = jnp.arange(LANES)
    ov = ((cstart[:, None] < (jb[None, :] + 1) * SEL_BLOCK)
          & (cstart[:, None] + CMP_BLOCK > jb[None, :] * SEL_BLOCK) & (jb[None, :] < nsel))
    return ov.astype(F32)


def _block_onehot(seq):
    blk = jnp.arange(seq)[:, None] // SEL_BLOCK
    return (blk == jnp.arange(LANES)[None, :]).astype(BF16)


def _layer(x, p, wts, layer, cosf, sinf, overlap, onehot, batch, seq):
    zc, zp, z, zg, grp = proj_in(x, p["attn_norm"], wts["w_in"], wts["w_nsa"], wts["w_gate"], layer,
                                 p["qk_conv_w"], p["qk_conv_b"], cosf, sinf, seq)

    nchunks = seq // MLSTM_CHUNK
    gate_rows = lambda c0: (zg[:, c0:c0 + MLSTM_HEADS].reshape(batch, seq, MLSTM_HEADS)
                            .transpose(0, 2, 1).reshape(batch, MLSTM_HEADS, nchunks, MLSTM_CHUNK))
    lanes = lambda v: jnp.broadcast_to(v.reshape(MLSTM_HEADS, 1, 1), (MLSTM_HEADS, 1, MLSTM_CHUNK))
    mix_a = mlstm_group(zc, zp, gate_rows(ZG_MI), gate_rows(ZG_MF), lanes(p["i_bias"]),
                        lanes(p["f_bias"]), p["mlstm_norm"], batch, seq)

    kcmp = compress(grp, 0, p["cmp_pe_k"], wts["cmp_w1_k"], wts["cmp_w2_k"], layer, batch, seq)
    vcmp = compress(grp, NSA_KV_HEADS, p["cmp_pe_v"], wts["cmp_w1_v"], wts["cmp_w2_v"], layer, batch, seq)
    o_cmp, sel = cmp_select(z, kcmp, vcmp, overlap, batch, seq)
    o_slc = sel_attn(z, onehot, sel, batch, seq)
    o_win = win_attn(z, batch, seq)
    mix_b = nsa_combine(o_cmp, o_slc, o_win, zg, p["nsa_norm"])

    x = matmul_res([mix_a, mix_b], wts["w_out"], layer, x, OUT_TM, D_MODEL)

    act = ffn_up(x, p["ffn_norm"], wts["w_up"], layer, p["ffn_conv_w"], p["ffn_conv_b"], seq)
    x = matmul_res([act], wts["w_down"], layer, x, MM_TM, DOWN_TN)
    return x


_LAYER_PARAMS = ("attn_norm", "w_in", "qk_conv_w", "qk_conv_b", "i_bias", "f_bias", "mlstm_norm",
                 "cmp_pe_k", "cmp_pe_v", "cmp_w1_k", "cmp_w2_k", "cmp_w1_v", "cmp_w2_v", "nsa_norm",
                 "w_out", "ffn_norm", "w_up", "ffn_conv_w", "ffn_conv_b", "w_down")


def kernel(x, attn_norm, w_in, qk_conv_w, qk_conv_b, i_bias, f_bias, mlstm_norm, cmp_pe_k, cmp_pe_v,
           cmp_w1_k, cmp_w2_k, cmp_w1_v, cmp_w2_v, nsa_norm, w_out, ffn_norm, w_up, ffn_conv_w,
           ffn_conv_b, w_down, final_norm):
    stacked = dict(zip(_LAYER_PARAMS, (attn_norm, w_in, qk_conv_w, qk_conv_b, i_bias, f_bias,
                                       mlstm_norm, cmp_pe_k, cmp_pe_v, cmp_w1_k, cmp_w2_k, cmp_w1_v,
                                       cmp_w2_v, nsa_norm, w_out, ffn_norm, w_up, ffn_conv_w,
                                       ffn_conv_b, w_down)))
    batch, seq, d = x.shape
    assert d == D_MODEL and seq % MM_TM == 0 and seq % ATTN_TQ == 0 and seq >= WIN_KEYS
    assert seq // CMP_STRIDE == LANES and seq // SEL_BLOCK <= LANES
    cosf, sinf = _rope_tables(seq)
    overlap = _overlap_matrix(seq, seq // CMP_STRIDE)
    onehot = _block_onehot(seq)
    big = ("w_in", "w_out", "w_up", "w_down", "cmp_w1_k", "cmp_w2_k", "cmp_w1_v", "cmp_w2_v")
    wts = {k: stacked[k].astype(BF16) for k in big}
    wts["w_nsa"] = wts["w_in"][..., W_IN_NSA[0]:W_IN_NSA[1]]
    wts["w_gate"] = _gate_weight(wts["w_in"])
    xf = x.reshape(batch * seq, d)
    for layer in range(DEPTH):
        p = {k: v[layer] for k, v in stacked.items() if k not in big}
        xf = _layer(xf, p, wts, layer, cosf, sinf, overlap, onehot, batch, seq)
    return rmsnorm(xf, final_norm).reshape(batch, seq, d)
```
